```python
import math
import jax, jax.numpy as jnp
from jax import lax
import numpy as np

D_MODEL = 1024
BATCH = 4
SEQ = 8192
DEPTH = 4

D_MIX = D_MODEL
FOX_HEADS = 8
FOX_HEAD_DIM = 64
FOX_W = FOX_HEADS * FOX_HEAD_DIM
GDN_HEADS = 4
GDN_DK = 128
GDN_DV = 128
GDN_WK = GDN_HEADS * GDN_DK
GDN_WV = GDN_HEADS * GDN_DV
CONV_K = 4
IN_DIM = 3 * FOX_W + FOX_HEADS + 2 * GDN_WK + GDN_WV + 2 * GDN_HEADS + GDN_WV
Q_BLOCK = 128
GDN_CHUNK = 64
D_FF = 2816
N_EXPERTS = 8
TOP_K = 2
D_FF_EXPERT = 3584
MOE_BLOCK = 512
N_DENSE = (DEPTH + 1) // 2
N_MOE = DEPTH // 2
EPS = 1e-6

kernel_name = "hybrid_fox_gdn_moe_trunk"


def rmsnorm(x, g):
    xf = x.astype(jnp.float32)
    y = xf * lax.rsqrt(jnp.mean(xf * xf, axis=-1, keepdims=True) + EPS)
    return (y * g.astype(jnp.float32)).astype(x.dtype)


def causal_depthwise_conv(x, w):
    c = x.shape[-1]
    return lax.conv_general_dilated(
        x, w[:, None, :].astype(x.dtype), window_strides=(1,), padding=[(CONV_K - 1, 0)],
        dimension_numbers=("NWC", "WIO", "NWC"), feature_group_count=c)


def forgetting_attention(q, k, v, f_logit, f_bias):
    B, T, H, dh = q.shape
    nb = T // Q_BLOCK
    scale = dh ** -0.5
    logf = jax.nn.log_sigmoid(f_logit.astype(jnp.float32) + f_bias.astype(jnp.float32))
    c = jnp.cumsum(logf, axis=1).transpose(0, 2, 1)
    qh = q.astype(jnp.float32).transpose(0, 2, 1, 3)
    kh = k.astype(jnp.float32).transpose(0, 2, 1, 3)
    vh = v.astype(jnp.float32).transpose(0, 2, 1, 3)
    qb = qh.reshape(B, H, nb, Q_BLOCK, dh).transpose(2, 0, 1, 3, 4)
    cb = c.reshape(B, H, nb, Q_BLOCK).transpose(2, 0, 1, 3)
    kpos = jnp.arange(T)

    def block(args):
        qi, ci, i = args
        s = jnp.einsum("bhqd,bhkd->bhqk", qi, kh) * scale + ci[..., None] - c[:, :, None, :]
        qpos = i * Q_BLOCK + jnp.arange(Q_BLOCK)
        s = jnp.where(kpos[None, :] <= qpos[:, None], s, -jnp.inf)
        p = jax.nn.softmax(s, axis=-1)
        return jnp.einsum("bhqk,bhkd->bhqd", p, vh)

    o = lax.map(block, (qb, cb, jnp.arange(nb)))
    o = o.transpose(1, 0, 3, 2, 4).reshape(B, T, H * dh)
    return o.astype(q.dtype)


def l2norm(x):
    xf = x.astype(jnp.float32)
    return xf * lax.rsqrt(jnp.sum(xf * xf, axis=-1, keepdims=True) + EPS)


def gated_delta_rule(q, k, v, g, beta):
    B, T, H, dk = q.shape
    dv = v.shape[-1]
    C = GDN_CHUNK
    N = T // C

    def chunk(t):
        t = jnp.moveaxis(t, 2, 1)
        return t.reshape((B, H, N, C) + t.shape[3:])

    q, k, v, g, beta = chunk(q), chunk(k), chunk(v), chunk(g), chunk(beta)
    g = jnp.cumsum(g, axis=-1)
    kb = k * beta[..., None]
    vb = v * beta[..., None]
    tril = jnp.tril(jnp.ones((C, C), dtype=bool))
    strict = jnp.tril(jnp.ones((C, C), dtype=bool), k=-1)
    diff = g[..., :, None] - g[..., None, :]
    L = jnp.where(tril, jnp.exp(jnp.where(tril, diff, 0.0)), 0.0)
    A = jnp.where(strict, jnp.einsum("bhncd,bhnsd->bhncs", kb, k) * L, 0.0)
    eye = jnp.eye(C, dtype=A.dtype)
    rhs = jnp.concatenate([vb, kb * jnp.exp(g)[..., None]], axis=-1)
    sol = lax.linalg.triangular_solve(A + eye, rhs, left_side=True, lower=True,
                                      unit_diagonal=True)
    u, w = sol[..., :dv], sol[..., dv:]
    intra = jnp.where(tril, jnp.einsum("bhncd,bhnsd->bhncs", q, k) * L, 0.0)

    def step(S, inp):
        qi, ki, ui, wi, gi, ai = inp
        v_new = ui - jnp.einsum("bhcd,bhde->bhce", wi, S)
        o = (jnp.einsum("bhcd,bhde->bhce", qi * jnp.exp(gi)[..., None], S)
             + jnp.einsum("bhcs,bhse->bhce", ai, v_new))
        glast = gi[..., -1]
        S = (S * jnp.exp(glast)[..., None, None]
             + jnp.einsum("bhcd,bhce->bhde", ki * jnp.exp(glast[..., None] - gi)[..., None], v_new))
        return S, o

    xs = tuple(jnp.moveaxis(t, 2, 0) for t in (q, k, u, w, g, intra))
    S0 = jnp.zeros((B, H, dk, dv), jnp.float32)
    _, o = lax.scan(step, S0, xs)
    return o.transpose(1, 0, 3, 2, 4).reshape(B, T, H, dv)


def hybrid_mixer(h, w_in, fox_f_bias, fox_norm_g, conv_w, a_log, dt_bias, gdn_norm_g, w_out):
    B, T, _ = h.shape
    z = jnp.einsum("btd,de->bte", h, w_in)
    sizes = [FOX_W, FOX_W, FOX_W, FOX_HEADS, 2 * GDN_WK + GDN_WV, GDN_HEADS, GDN_HEADS, GDN_WV]
    idx = np.cumsum(sizes)[:-1].tolist()
    fq, fk, fv, ff, gqkv, ga, gb, gz = jnp.split(z, idx, axis=-1)

    hs = (B, T, FOX_HEADS, FOX_HEAD_DIM)
    fo = forgetting_attention(fq.reshape(hs), fk.reshape(hs), fv.reshape(hs), ff, fox_f_bias)
    fo = rmsnorm(fo.reshape(hs), fox_norm_g.reshape(FOX_HEADS, FOX_HEAD_DIM)).reshape(B, T, FOX_W)

    gqkv = jax.nn.silu(causal_depthwise_conv(gqkv, conv_w))
    gq, gk, gv = jnp.split(gqkv, [GDN_WK, 2 * GDN_WK], axis=-1)
    gq = l2norm(gq.reshape(B, T, GDN_HEADS, GDN_DK)) * (GDN_DK ** -0.5)
    gk = l2norm(gk.reshape(B, T, GDN_HEADS, GDN_DK))
    gv = gv.reshape(B, T, GDN_HEADS, GDN_DV).astype(jnp.float32)
    g = -jnp.exp(a_log.astype(jnp.float32)) * jax.nn.softplus(ga.astype(jnp.float32) + dt_bias.astype(jnp.float32))
    beta = jax.nn.sigmoid(gb.astype(jnp.float32))
    go = gated_delta_rule(gq, gk, gv, g, beta)
    gate = jax.nn.silu(gz.reshape(B, T, GDN_HEADS, GDN_DV).astype(jnp.float32))
    go = (rmsnorm(go, gdn_norm_g) * gate).reshape(B, T, GDN_WV).astype(h.dtype)

    mix = jnp.concatenate([fo, go], axis=-1)
    return jnp.einsum("bte,ed->btd", mix, w_out)


def swiglu(h, w1, w3, w2):
    a = jnp.einsum("...d,df->...f", h, w1)
    b = jnp.einsum("...d,df->...f", h, w3)
    return jnp.einsum("...f,fd->...d", jax.nn.silu(a) * b, w2)


def moe_swiglu(h, router_w, w1, w3, w2):
    B, T, D = h.shape
    Ntok = B * T
    xf = h.reshape(Ntok, D)
    logits = jnp.einsum("nd,de->ne", xf, router_w).astype(jnp.float32)
    probs = jax.nn.softmax(logits, axis=-1)
    topv, topi = lax.top_k(probs, TOP_K)
    topv = topv / jnp.sum(topv, axis=-1, keepdims=True)
    A = Ntok * TOP_K
    flat_e = topi.reshape(A)
    flat_w = topv.reshape(A)
    flat_tok = jnp.arange(A, dtype=jnp.int32) // TOP_K
    order = jnp.argsort(flat_e)
    se = flat_e[order]
    counts = jnp.bincount(flat_e, length=N_EXPERTS)
    start = jnp.cumsum(counts) - counts
    padded = ((counts + MOE_BLOCK - 1) // MOE_BLOCK) * MOE_BLOCK
    pend = jnp.cumsum(padded)
    pstart = pend - padded
    dest = pstart[se] + jnp.arange(A) - start[se]
    n_blocks = -(-A // MOE_BLOCK) + N_EXPERTS
    P = n_blocks * MOE_BLOCK
    slot_tok = jnp.zeros((P,), jnp.int32).at[dest].set(flat_tok[order])
    slot_w = jnp.zeros((P,), jnp.float32).at[dest].set(flat_w[order])
    block_e = jnp.clip(jnp.searchsorted(pend, jnp.arange(n_blocks) * MOE_BLOCK, side="right"),
                       0, N_EXPERTS - 1)
    xs = xf[slot_tok].reshape(n_blocks, MOE_BLOCK, D)

    def expert_block(args):
        xb, e = args
        return swiglu(xb, w1[e], w3[e], w2[e])

    ys = lax.map(expert_block, (xs, block_e)).reshape(P, D)
    out = jnp.zeros((Ntok, D), ys.dtype).at[slot_tok].add(ys * slot_w[:, None].astype(ys.dtype))
    return out.reshape(B, T, D)


def setup_inputs(seed: int = 0) -> dict:
    key = jax.random.key(seed)
    ks = jax.random.split(key, 20)
    f32 = jnp.float32
    nrm = lambda k, s, sc: jax.random.normal(k, s, f32) * sc
    res_scale = (2 * DEPTH) ** -0.5
    x = jax.random.normal(ks[0], (BATCH, SEQ, D_MODEL), f32)
    ln1_g = 1.0 + nrm(ks[1], (DEPTH, D_MODEL), 0.02)
    w_in = nrm(ks[2], (DEPTH, D_MODEL, IN_DIM), D_MODEL ** -0.5)
    fox_f_bias = jax.random.uniform(ks[3], (DEPTH, FOX_HEADS), f32, 2.0, 7.0)
    fox_norm_g = 1.0 + nrm(ks[4], (DEPTH, FOX_W), 0.02)
    gdn_conv_w = nrm(ks[5], (DEPTH, CONV_K, 2 * GDN_WK + GDN_WV), CONV_K ** -0.5)
    gdn_a_log = jnp.log(jax.random.uniform(ks[6], (DEPTH, GDN_HEADS), f32, 1.0, 16.0))
    dt = jnp.exp(jax.random.uniform(ks[7], (DEPTH, GDN_HEADS), f32, math.log(1e-3), math.log(1e-1)))
    gdn_dt_bias = dt + jnp.log(-jnp.expm1(-dt))
    gdn_norm_g = 1.0 + nrm(ks[8], (DEPTH, GDN_DV), 0.02)
    w_out = nrm(ks[9], (DEPTH, D_MIX, D_MODEL), D_MIX ** -0.5 * res_scale)
    ln2_g = 1.0 + nrm(ks[10], (DEPTH, D_MODEL), 0.02)
    ffn_w1 = nrm(ks[11], (N_DENSE, D_MODEL, D_FF), D_MODEL ** -0.5)
    ffn_w3 = nrm(ks[12], (N_DENSE, D_MODEL, D_FF), D_MODEL ** -0.5)
    ffn_w2 = nrm(ks[13], (N_DENSE, D_FF, D_MODEL), D_FF ** -0.5 * res_scale)
    router_w = nrm(ks[14], (N_MOE, D_MODEL, N_EXPERTS), D_MODEL ** -0.5)
    exp_w1 = nrm(ks[15], (N_MOE, N_EXPERTS, D_MODEL, D_FF_EXPERT), D_MODEL ** -0.5)
    exp_w3 = nrm(ks[16], (N_MOE, N_EXPERTS, D_MODEL, D_FF_EXPERT), D_MODEL ** -0.5)
    exp_w2 = nrm(ks[17], (N_MOE, N_EXPERTS, D_FF_EXPERT, D_MODEL), D_FF_EXPERT ** -0.5 * res_scale)
    final_g = 1.0 + nrm(ks[18], (D_MODEL,), 0.02)
    return {"x": x, "ln1_g": ln1_g, "w_in": w_in, "fox_f_bias": fox_f_bias,
            "fox_norm_g": fox_norm_g, "gdn_conv_w": gdn_conv_w, "gdn_a_log": gdn_a_log,
            "gdn_dt_bias": gdn_dt_bias, "gdn_norm_g": gdn_norm_g, "w_out": w_out,
            "ln2_g": ln2_g, "ffn_w1": ffn_w1, "ffn_w3": ffn_w3, "ffn_w2": ffn_w2,
            "router_w": router_w, "exp_w1": exp_w1, "exp_w3": exp_w3, "exp_w2": exp_w2,
            "final_g": final_g}


def reference(x, ln1_g, w_in, fox_f_bias, fox_norm_g, gdn_conv_w, gdn_a_log, gdn_dt_bias,
              gdn_norm_g, w_out, ln2_g, ffn_w1, ffn_w3, ffn_w2, router_w, exp_w1, exp_w3,
              exp_w2, final_g):
    for layer in range(DEPTH):
        h = rmsnorm(x, ln1_g[layer])
        x = x + hybrid_mixer(h, w_in[layer], fox_f_bias[layer], fox_norm_g[layer],
                             gdn_conv_w[layer], gdn_a_log[layer], gdn_dt_bias[layer],
                             gdn_norm_g[layer], w_out[layer])
        h = rmsnorm(x, ln2_g[layer])
        j = layer // 2
        if layer % 2 == 0:
            x = x + swiglu(h, ffn_w1[j], ffn_w3[j], ffn_w2[j])
        else:
            x = x + moe_swiglu(h, router_w[j], exp_w1[j], exp_w3[j], exp_w2[j])
    return rmsnorm(x, final_g)
```

```python
import functools

import jax
import jax.numpy as jnp
from jax import lax
from jax.experimental import pallas as pl
from jax.experimental.pallas import tpu as pltpu

F32 = jnp.float32
BF16 = jnp.bfloat16
HI = lax.Precision.HIGHEST

D_MODEL = 1024
FOX_HEADS = 8
FOX_HEAD_DIM = 64
FOX_W = FOX_HEADS * FOX_HEAD_DIM
GDN_HEADS = 4
GDN_DK = 128
GDN_DV = 128
GDN_W = GDN_HEADS * GDN_DK
CONV_K = 4
GDN_CHUNK = 64
N_EXPERTS = 8
MOE_BLOCK = 512
EPS = 1e-6

LANES = 128
NEG_BIG = -1e30
MIB = 1024 * 1024

REST_QKV = 3 * GDN_W
REST_Z = REST_QKV
REST_GATE = REST_QKV + GDN_W
REST_W = REST_GATE + LANES
GATE_BLK = REST_GATE // LANES
G_FOX = 0
G_DEC = FOX_HEADS
G_BETA = FOX_HEADS + GDN_HEADS
G_ROWS = FOX_HEADS + 2 * GDN_HEADS


def _dot(a, b, precision=None):
    return jnp.dot(a, b, preferred_element_type=F32, precision=precision)


def _dot_nt(a, b, precision=None):
    return lax.dot_general(a, b, (((1,), (1,)), ((), ())),
                           preferred_element_type=F32, precision=precision)


def _dot_tn(a, b, precision=None):
    return lax.dot_general(a, b, (((0,), (0,)), ((), ())),
                           preferred_element_type=F32, precision=precision)


def _params(semantics, vmem_mib):
    return pltpu.CompilerParams(dimension_semantics=semantics,
                                vmem_limit_bytes=vmem_mib * MIB)


def _rms(x, g):
    return x * lax.rsqrt(jnp.mean(x * x, axis=-1, keepdims=True) + EPS) * g


def _silu(x):
    return x * jax.nn.sigmoid(x)


def _resident(shape):
    nd = len(shape)
    return pl.BlockSpec(shape, lambda *_: (0,) * nd, pipeline_mode=pl.Buffered(1))


def _rms_body(x_ref, g_ref, o_ref):
    o_ref[...] = _rms(x_ref[...], g_ref[...]).astype(o_ref.dtype)


def _rmsnorm_rows(x, g, out_dtype, tm=1024):
    n, d = x.shape
    return pl.pallas_call(
        _rms_body,
        grid=(n // tm,),
        in_specs=[pl.BlockSpec((tm, d), lambda i: (i, 0)),
                  pl.BlockSpec((1, d), lambda i: (0, 0))],
        out_specs=pl.BlockSpec((tm, d), lambda i: (i, 0)),
        out_shape=jax.ShapeDtypeStruct((n, d), out_dtype),
        compiler_params=_params(("parallel",), 32),
        name="rmsnorm",
    )(x, g.reshape(1, d))


def _inproj_body(h_ref, w_ref, ofox_ref, orest_ref, *, col_chunk):
    h = h_ref[...]
    nf = ofox_ref.shape[1]
    nr = orest_ref.shape[1]
    for c in range(0, nf, col_chunk):
        ofox_ref[:, c:c + col_chunk] = _dot(h, w_ref[:, c:c + col_chunk]).astype(ofox_ref.dtype)
    for c in range(0, nr, col_chunk):
        e = min(c + col_chunk, nr)
        orest_ref[:, c:e] = _dot(h, w_ref[:, nf + c:nf + e])


def _inproj(h, w_all, tm=512):
    n, d = h.shape
    nf, nr = 3 * FOX_W, REST_W
    return pl.pallas_call(
        functools.partial(_inproj_body, col_chunk=512),
        grid=(n // tm,),
        in_specs=[pl.BlockSpec((tm, d), lambda i: (i, 0)),
                  _resident((d, nf + nr))],
        out_specs=[pl.BlockSpec((tm, nf), lambda i: (i, 0)),
                   pl.BlockSpec((tm, nr), lambda i: (i, 0))],
        out_shape=[jax.ShapeDtypeStruct((n, nf), BF16),
                   jax.ShapeDtypeStruct((n, nr), F32)],
        compiler_params=_params(("parallel",), 40),
        name="inproj",
    )(h, w_all)


def _gates_body(z_ref, p_ref, o_ref, ot_ref, carry_ref, *, tt):
    @pl.when(pl.program_id(1) == 0)
    def _():
        carry_ref[...] = jnp.zeros_like(carry_ref)

    z = z_ref[0] + p_ref[0:1, :]
    lane = lax.broadcasted_iota(jnp.int32, z.shape, 1)
    tail = jnp.log(1.0 + jnp.exp(-jnp.abs(z)))
    log_sig = jnp.minimum(z, 0.0) - tail
    softplus = jnp.maximum(z, 0.0) + tail
    decay = -jnp.exp(p_ref[1:2, :]) * softplus
    val = jnp.where(lane < G_DEC, log_sig, jnp.where(lane < G_BETA, decay, jax.nn.sigmoid(z)))

    ri = lax.broadcasted_iota(jnp.int32, (tt, tt), 0)
    ci = lax.broadcasted_iota(jnp.int32, (tt, tt), 1)
    tri = jnp.where(ci <= ri, 1.0, 0.0).astype(F32)
    blk = jnp.where((ci <= ri) & (ri // GDN_CHUNK == ci // GDN_CHUNK), 1.0, 0.0).astype(F32)
    full_cum = _dot(tri, val, HI) + carry_ref[...]
    chunk_cum = _dot(blk, val, HI)
    out = jnp.where(lane < G_DEC, full_cum, jnp.where(lane < G_BETA, chunk_cum, val))
    carry_ref[...] = full_cum[tt - 1:tt, :]
    o_ref[0] = out
    ot_ref[0] = out.T[:G_ROWS, :]


def _gates(rest3, gate_params, tt=512):
    b, t, _ = rest3.shape
    return pl.pallas_call(
        functools.partial(_gates_body, tt=tt),
        grid=(b, t // tt),
        in_specs=[pl.BlockSpec((1, tt, LANES), lambda i, j: (i, j, GATE_BLK)),
                  pl.BlockSpec((8, LANES), lambda i, j: (0, 0))],
        out_specs=[pl.BlockSpec((1, tt, LANES), lambda i, j: (i, j, 0)),
                   pl.BlockSpec((1, G_ROWS, tt), lambda i, j: (i, 0, j))],
        out_shape=[jax.ShapeDtypeStruct((b, t, LANES), F32),
                   jax.ShapeDtypeStruct((b, G_ROWS, t), F32)],
        scratch_shapes=[pltpu.VMEM((1, LANES), F32)],
        compiler_params=_params(("parallel", "arbitrary"), 32),
        name="gates",
    )(rest3, gate_params)


def _fox_body(q_ref, k_ref, v_ref, cq_ref, ck_ref, g_ref, o_ref,
              m_ref, l_ref, acc_ref, *, tq):
    hp = pl.program_id(1)
    qi = pl.program_id(2)
    half = FOX_HEAD_DIM
    lane_row = lax.broadcasted_iota(jnp.int32, (1, LANES), 1)
    lo = lane_row < half

    q = q_ref[0]
    zero = jnp.zeros_like(q)
    q_heads = (jnp.where(lo, q, zero), jnp.where(lo, zero, q))

    cq_tile = cq_ref[0]
    lane_t = lax.broadcasted_iota(jnp.int32, cq_tile.shape, 1)
    cq_heads = tuple(
        jnp.sum(jnp.where(lane_t == G_FOX + 2 * hp + j, cq_tile, 0.0), axis=1, keepdims=True)
        for j in range(2))

    m_ref[...] = jnp.full(m_ref.shape, NEG_BIG, F32)
    l_ref[...] = jnp.zeros_like(l_ref)
    acc_ref[...] = jnp.zeros_like(acc_ref)

    def kv_step(ki, diagonal):
        ks = pl.multiple_of(ki * tq, tq)
        k = k_ref[0, pl.ds(ks, tq), :]
        v = v_ref[0, pl.ds(ks, tq), :]
        acc = acc_ref[...]
        new_acc = []
        for j in range(2):
            ck = ck_ref[0, 0, j:j + 1, pl.ds(ks, tq)]
            s = _dot_nt(q_heads[j], k) - ck
            if diagonal:
                ri = lax.broadcasted_iota(jnp.int32, s.shape, 0)
                ci = lax.broadcasted_iota(jnp.int32, s.shape, 1)
                s = jnp.where(ci <= ri, s, NEG_BIG)
            m_old = m_ref[j]
            m_new = jnp.maximum(m_old, cq_heads[j] + jnp.max(s, axis=1, keepdims=True))
            p = jnp.exp(s + (cq_heads[j] - m_new))
            alpha = jnp.exp(m_old - m_new)
            l_ref[j] = alpha * l_ref[j] + jnp.sum(p, axis=1, keepdims=True)
            m_ref[j] = m_new
            new_acc.append(alpha * acc + _dot(p.astype(BF16), v))
        acc_ref[...] = jnp.where(lo, new_acc[0], new_acc[1])

    def off_diag(ki, carry):
        kv_step(ki, False)
        return carry

    lax.fori_loop(0, qi, off_diag, 0)
    kv_step(qi, True)

    o = acc_ref[...] / jnp.where(lo, l_ref[0], l_ref[1])
    sq = o * o
    ms0 = jnp.sum(jnp.where(lo, sq, 0.0), axis=1, keepdims=True) / half
    ms1 = jnp.sum(jnp.where(lo, 0.0, sq), axis=1, keepdims=True) / half
    inv = lax.rsqrt(jnp.where(lo, ms0, ms1) + EPS)
    o_ref[0] = (o * inv * g_ref[...]).astype(o_ref.dtype)


def _fox(fox3, gates3, gates_t, norm_g, tq=512):
    b, t, _ = fox3.shape
    npair = FOX_HEADS // 2
    ck4 = gates_t.reshape(b, G_ROWS // 2, 2, t)
    return pl.pallas_call(
        functools.partial(_fox_body, tq=tq),
        grid=(b, npair, t // tq),
        in_specs=[pl.BlockSpec((1, tq, LANES), lambda i, p, j: (i, j, p)),
                  pl.BlockSpec((1, t, LANES), lambda i, p, j: (i, 0, npair + p)),
                  pl.BlockSpec((1, t, LANES), lambda i, p, j: (i, 0, 2 * npair + p)),
                  pl.BlockSpec((1, tq, LANES), lambda i, p, j: (i, j, 0)),
                  pl.BlockSpec((1, 1, 2, t), lambda i, p, j: (i, p, 0, 0)),
                  pl.BlockSpec((1, LANES), lambda i, p, j: (0, p))],
        out_specs=pl.BlockSpec((1, tq, LANES), lambda i, p, j: (i, j, p)),
        out_shape=jax.ShapeDtypeStruct((b, t, FOX_W), BF16),
        scratch_shapes=[pltpu.VMEM((2, tq, 1), F32),
                        pltpu.VMEM((2, tq, 1), F32),
                        pltpu.VMEM((tq, LANES), F32)],
        compiler_params=_params(("parallel", "parallel", "arbitrary"), 48),
        name="fox_attention",
    )(fox3, fox3, fox3, gates3, ck4, norm_g.reshape(1, FOX_W))


SUPER = 256


def _gdn_body(xq_ref, xk_ref, xv_ref, z_ref, gc_ref, gr_ref, wq_ref, wk_ref, wv_ref, ng_ref,
              o_ref, s_ref, halo_ref, buf_ref, obuf_ref, *, tt):
    h = pl.program_id(1)
    halo_rows = 8

    @pl.when(pl.program_id(2) == 0)
    def _():
        s_ref[...] = jnp.zeros_like(s_ref)
        halo_ref[...] = jnp.zeros_like(halo_ref)

    def conv_silu(x_ref, idx, w_ref):
        x = x_ref[0]
        buf_ref[0:halo_rows, :] = halo_ref[idx]
        buf_ref[halo_rows:halo_rows + tt, :] = x
        halo_ref[idx] = x[tt - halo_rows:tt, :]
        y = jnp.zeros_like(x)
        for j in range(CONV_K):
            off = halo_rows - (CONV_K - 1) + j
            y = y + w_ref[j:j + 1, :] * buf_ref[off:off + tt, :]
        return _silu(y)

    q = conv_silu(xq_ref, 0, wq_ref)
    k = conv_silu(xk_ref, 1, wk_ref)
    v = conv_silu(xv_ref, 2, wv_ref)
    q = q * lax.rsqrt(jnp.sum(q * q, axis=-1, keepdims=True) + EPS) * (GDN_DK ** -0.5)
    k = k * lax.rsqrt(jnp.sum(k * k, axis=-1, keepdims=True) + EPS)

    gates = gc_ref[0]
    lane_t = lax.broadcasted_iota(jnp.int32, gates.shape, 1)
    gcol = jnp.sum(jnp.where(lane_t == G_DEC + h, gates, 0.0), axis=1, keepdims=True)
    bcol = jnp.sum(jnp.where(lane_t == G_BETA + h, gates, 0.0), axis=1, keepdims=True)
    grow = gr_ref[0, 0]
    eg = jnp.exp(gcol)
    kb = k * bcol
    vb = v * bcol
    kbe = kb * eg
    qe = q * eg

    ri = lax.broadcasted_iota(jnp.int32, (SUPER, SUPER), 0)
    ci = lax.broadcasted_iota(jnp.int32, (SUPER, SUPER), 1)
    same = (ri // GDN_CHUNK) == (ci // GDN_CHUNK)
    tril = same & (ci <= ri)
    strict = same & (ci < ri)
    eye = jnp.where(ri == ci, 1.0, 0.0).astype(F32)

    for sc in range(tt // SUPER):
        r0 = sc * SUPER
        rows = slice(r0, r0 + SUPER)
        diff = gcol[rows] - grow[:, rows]
        decay = jnp.where(tril, jnp.exp(jnp.where(tril, diff, 0.0)), 0.0)
        a = jnp.where(strict, _dot_nt(kb[rows], k[rows], HI) * decay, 0.0)
        inv = eye - a
        pw = a
        for _ in range(5):
            pw = _dot(pw, pw, HI)
            inv = inv + _dot(inv, pw, HI)
        uw = _dot(inv, jnp.concatenate([vb[rows], kbe[rows]], axis=1), HI)
        u = uw[:, :GDN_DV]
        w = uw[:, GDN_DV:]
        intra = jnp.where(tril, _dot_nt(q[rows], k[rows], HI) * decay, 0.0)
        for c in range(SUPER // GDN_CHUNK):
            lr = slice(c * GDN_CHUNK, (c + 1) * GDN_CHUNK)
            gr = slice(r0 + c * GDN_CHUNK, r0 + (c + 1) * GDN_CHUNK)
            s = s_ref[...]
            v_new = u[lr] - _dot(w[lr], s, HI)
            obuf_ref[gr, :] = _dot(qe[gr], s, HI) + _dot(intra[lr, lr], v_new, HI)
            g_last = gcol[gr.stop - 1:gr.stop, :]
            k_dec = k[gr] * jnp.exp(g_last - gcol[gr])
            s_ref[...] = s * jnp.exp(g_last) + _dot_tn(k_dec, v_new, HI)

    o = obuf_ref[...]
    o = _rms(o, ng_ref[...]) * _silu(z_ref[0])
    o_ref[0] = o.astype(o_ref.dtype)


def _gdn(rest3, gates3, gates_t, conv_w, norm_g, tt=512):
    b, t, _ = rest3.shape
    nh = GDN_HEADS
    gr4 = gates_t.reshape(b, G_ROWS, 1, t)
    x_spec = lambda off: pl.BlockSpec((1, tt, LANES), lambda i, h, j: (i, j, off + h))
    w_spec = lambda off: pl.BlockSpec((CONV_K, LANES), lambda i, h, j: (0, off + h))
    return pl.pallas_call(
        functools.partial(_gdn_body, tt=tt),
        grid=(b, nh, t // tt),
        in_specs=[x_spec(0), x_spec(nh), x_spec(2 * nh), x_spec(3 * nh),
                  pl.BlockSpec((1, tt, LANES), lambda i, h, j: (i, j, 0)),
                  pl.BlockSpec((1, 1, 1, tt), lambda i, h, j: (i, G_DEC + h, 0, j)),
                  w_spec(0), w_spec(nh), w_spec(2 * nh),
                  pl.BlockSpec((1, LANES), lambda i, h, j: (0, 0))],
        out_specs=pl.BlockSpec((1, tt, LANES), lambda i, h, j: (i, j, h)),
        out_shape=jax.ShapeDtypeStruct((b, t, GDN_W), BF16),
        scratch_shapes=[pltpu.VMEM((GDN_DK, GDN_DV), F32),
                        pltpu.VMEM((3, 8, LANES), F32),
                        pltpu.VMEM((tt + 8, LANES), F32),
                        pltpu.VMEM((tt, LANES), F32)],
        compiler_params=_params(("parallel", "parallel", "arbitrary"), 48),
        name="gated_delta",
    )(rest3, rest3, rest3, rest3, gates3, gr4, conv_w, conv_w, conv_w, norm_g.reshape(1, GDN_DV))


def _outproj_body(x_ref, fo_ref, go_ref, w_ref, g_ref, xo_ref, *maybe_h_ref):
    y = _dot(fo_ref[...], w_ref[0:FOX_W, :]) + _dot(go_ref[...], w_ref[FOX_W:FOX_W + GDN_W, :])
    xn = x_ref[...] + y
    xo_ref[...] = xn
    if maybe_h_ref:
        maybe_h_ref[0][...] = _rms(xn, g_ref[...]).astype(BF16)


def _outproj(x, fo, go, w_out, ln_g, want_h, tm=512):
    n, d = x.shape
    row = lambda w: pl.BlockSpec((tm, w), lambda i: (i, 0))
    out_specs = [row(d)]
    out_shape = [jax.ShapeDtypeStruct((n, d), F32)]
    if want_h:
        out_specs.append(row(d))
        out_shape.append(jax.ShapeDtypeStruct((n, d), BF16))
    res = pl.pallas_call(
        _outproj_body,
        grid=(n // tm,),
        in_specs=[row(d), row(FOX_W), row(GDN_W), _resident((FOX_W + GDN_W, d)),
                  pl.BlockSpec((1, d), lambda i: (0, 0))],
        out_specs=out_specs,
        out_shape=out_shape,
        compiler_params=_params(("parallel",), 32),
        name="outproj",
    )(x, fo, go, w_out, ln_g.reshape(1, d))
    return res if want_h else (res[0], None)


def _ffn_body(x_ref, h_ref, w1_ref, w3_ref, w2_ref, g_ref, xo_ref, ho_ref, acc_ref, *, f_chunk):
    h = h_ref[...]
    d_ff = w1_ref.shape[1]
    for i, c in enumerate(range(0, d_ff, f_chunk)):
        a = _dot(h, w1_ref[:, c:c + f_chunk])
        b = _dot(h, w3_ref[:, c:c + f_chunk])
        y = _dot((_silu(a) * b).astype(BF16), w2_ref[c:c + f_chunk, :])
        if i == 0:
            acc_ref[...] = y
        else:
            acc_ref[...] += y
    xn = x_ref[...] + acc_ref[...]
    xo_ref[...] = xn
    ho_ref[...] = _rms(xn, g_ref[...]).astype(ho_ref.dtype)


def _ffn(x, h, w1, w3, w2, next_g, tm=512, f_chunk=256):
    n, d = x.shape
    d_ff = w1.shape[1]
    row = pl.BlockSpec((tm, d), lambda i: (i, 0))
    return pl.pallas_call(
        functools.partial(_ffn_body, f_chunk=f_chunk),
        grid=(n // tm,),
        in_specs=[row, row, _resident((d, d_ff)), _resident((d, d_ff)), _resident((d_ff, d)),
                  pl.BlockSpec((1, d), lambda i: (0, 0))],
        out_specs=[row, row],
        out_shape=[jax.ShapeDtypeStruct((n, d), F32), jax.ShapeDtypeStruct((n, d), BF16)],
        scratch_shapes=[pltpu.VMEM((tm, d), F32)],
        compiler_params=_params(("parallel",), 52),
        name="dense_swiglu",
    )(x, h, w1, w3, w2, next_g.reshape(1, d))


R_E0, R_E1, R_W0, R_W1, R_RANK0, R_RANK1 = 0, 1, 2, 3, 4, 5


def _router_body(x_ref, g_ref, w_ref, info_ref, cnt_ref, carry_ref, *, tm):
    @pl.when(pl.program_id(0) == 0)
    def _():
        carry_ref[...] = jnp.zeros_like(carry_ref)

    h = _rms(x_ref[...], g_ref[...])
    logits = _dot(h, w_ref[...], HI)
    lane = lax.broadcasted_iota(jnp.int32, logits.shape, 1)
    valid = lane < N_EXPERTS
    lm = jnp.where(valid, logits, NEG_BIG)
    ex = jnp.exp(lm - jnp.max(lm, axis=1, keepdims=True))
    probs = jnp.where(valid, ex / jnp.sum(ex, axis=1, keepdims=True), -1.0)
    p0 = jnp.max(probs, axis=1, keepdims=True)
    e0 = jnp.min(jnp.where(probs == p0, lane, LANES), axis=1, keepdims=True)
    rest = jnp.where(lane == e0, -1.0, probs)
    p1 = jnp.max(rest, axis=1, keepdims=True)
    e1 = jnp.min(jnp.where(rest == p1, lane, LANES), axis=1, keepdims=True)
    denom = p0 + p1

    picked = jnp.where((lane == e0) | (lane == e1), 1.0, 0.0).astype(F32)
    ri = lax.broadcasted_iota(jnp.int32, (tm, tm), 0)
    ci = lax.broadcasted_iota(jnp.int32, (tm, tm), 1)
    before = jnp.where(ci < ri, 1.0, 0.0).astype(BF16)
    earlier = _dot(before, picked.astype(BF16)) + carry_ref[...]
    rank0 = jnp.sum(jnp.where(lane == e0, earlier, 0.0), axis=1, keepdims=True)
    rank1 = jnp.sum(jnp.where(lane == e1, earlier, 0.0), axis=1, keepdims=True)
    carry_ref[...] += jnp.sum(picked, axis=0, keepdims=True)

    info = jnp.zeros(logits.shape, F32)
    for col, val in ((R_E0, e0.astype(F32)), (R_E1, e1.astype(F32)), (R_W0, p0 / denom),
                     (R_W1, p1 / denom), (R_RANK0, rank0), (R_RANK1, rank1)):
        info = jnp.where(lane == col, val, info)
    info_ref[...] = info
    cnt_ref[...] = jnp.broadcast_to(carry_ref[...], cnt_ref.shape)


def _router(x, ln_g, router_w_pad, tm=512):
    n, d = x.shape
    return pl.pallas_call(
        functools.partial(_router_body, tm=tm),
        grid=(n // tm,),
        in_specs=[pl.BlockSpec((tm, d), lambda i: (i, 0)),
                  pl.BlockSpec((1, d), lambda i: (0, 0)),
                  pl.BlockSpec((d, LANES), lambda i: (0, 0))],
        out_specs=[pl.BlockSpec((tm, LANES), lambda i: (i, 0)),
                   pl.BlockSpec((8, LANES), lambda i: (0, 0))],
        out_shape=[jax.ShapeDtypeStruct((n, LANES), F32),
                   jax.ShapeDtypeStruct((8, LANES), F32)],
        scratch_shapes=[pltpu.VMEM((1, LANES), F32)],
        compiler_params=_params(("arbitrary",), 32),
        name="moe_router",
    )(x, ln_g.reshape(1, d), router_w_pad)


def _row_copy(src_ref, src_row, dst_ref, dst_row, sem):
    return pltpu.make_async_copy(src_ref.at[pl.ds(src_row, 1)], dst_ref.at[pl.ds(dst_row, 1)], sem)


def _dispatch_body(pos_ref, x_ref, g_ref, xs_in_ref, xs_ref, buf_ref, sem_ref, *, tm):
    del xs_in_ref
    i = pl.program_id(0)
    slot = i % 2
    buf_ref[slot] = _rms(x_ref[...], g_ref[...])

    def issue(r, carry):
        base = 2 * (i * tm + r)
        for j in range(2):
            _row_copy(buf_ref.at[slot], r, xs_ref, pos_ref[base + j], sem_ref.at[slot]).start()
        return carry

    lax.fori_loop(0, tm, issue, 0)

    def wait_slot(s):
        for _ in range(2):
            pltpu.make_async_copy(buf_ref.at[s], xs_ref.at[pl.ds(0, tm)], sem_ref.at[s]).wait()

    @pl.when(i > 0)
    def _():
        wait_slot(1 - slot)

    @pl.when(i == pl.num_programs(0) - 1)
    def _():
        wait_slot(slot)


def _dispatch(x, ln_g, pos_flat, n_slots, tm=256):
    n, d = x.shape
    grid_spec = pltpu.PrefetchScalarGridSpec(
        num_scalar_prefetch=1,
        grid=(n // tm,),
        in_specs=[pl.BlockSpec((tm, d), lambda i, pos: (i, 0)),
                  pl.BlockSpec((1, d), lambda i, pos: (0, 0)),
                  pl.BlockSpec(memory_space=pl.ANY)],
        out_specs=pl.BlockSpec(memory_space=pl.ANY),
        scratch_shapes=[pltpu.VMEM((2, tm, d), F32), pltpu.SemaphoreType.DMA((2,))],
    )
    return pl.pallas_call(
        functools.partial(_dispatch_body, tm=tm),
        grid_spec=grid_spec,
        out_shape=jax.ShapeDtypeStruct((n_slots, d), F32),
        input_output_aliases={3: 0},
        compiler_params=_params(("arbitrary",), 32),
        name="moe_dispatch",
    )(pos_flat, x, ln_g.reshape(1, d), jnp.zeros((n_slots, d), F32))


def _experts_body(be_ref, nu_ref, xs_ref, w1_ref, w3_ref, w2_ref, o_ref, xb_ref):
    b = pl.program_id(0)
    f = pl.program_id(1)
    used = b < nu_ref[0]

    @pl.when(used & (f == 0))
    def _():
        xb_ref[...] = xs_ref[...].astype(BF16)

    @pl.when(used)
    def _():
        xb = xb_ref[...]
        a = _dot(xb, w1_ref[0])
        g = _dot(xb, w3_ref[0])
        y = _dot((_silu(a) * g).astype(BF16), w2_ref[0])

        @pl.when(f == 0)
        def _():
            o_ref[...] = y

        @pl.when(f > 0)
        def _():
            o_ref[...] += y

    @pl.when(jnp.logical_not(used) & (f == 0))
    def _():
        o_ref[...] = jnp.zeros_like(o_ref)


def _experts(xs, block_e, n_used, w1, w3, w2, f_chunk=896):
    p, d = xs.shape
    d_ff = w1.shape[2]
    nb = p // MOE_BLOCK
    grid_spec = pltpu.PrefetchScalarGridSpec(
        num_scalar_prefetch=2,
        grid=(nb, d_ff // f_chunk),
        in_specs=[pl.BlockSpec((MOE_BLOCK, d), lambda b, f, be, nu: (b, 0)),
                  pl.BlockSpec((1, d, f_chunk), lambda b, f, be, nu: (be[b], 0, f)),
                  pl.BlockSpec((1, d, f_chunk), lambda b, f, be, nu: (be[b], 0, f)),
                  pl.BlockSpec((1, f_chunk, d), lambda b, f, be, nu: (be[b], f, 0))],
        out_specs=pl.BlockSpec((MOE_BLOCK, d), lambda b, f, be, nu: (b, 0)),
        scratch_shapes=[pltpu.VMEM((MOE_BLOCK, d), BF16)],
    )
    return pl.pallas_call(
        _experts_body,
        grid_spec=grid_spec,
        out_shape=jax.ShapeDtypeStruct((p, d), F32),
        compiler_params=_params(("arbitrary", "arbitrary"), 48),
        name="moe_experts",
    )(block_e, n_used, xs, w1, w3, w2)


def _combine_body(pos_ref, x_ref, info_ref, g_ref, ys_ref, xo_ref, ho_ref, y_ref, sem_ref, *, tm):
    i = pl.program_id(0)

    def issue(r, carry):
        base = 2 * (i * tm + r)
        for j in range(2):
            _row_copy(ys_ref, pos_ref[base + j], y_ref.at[j], r, sem_ref.at[j]).start()
        return carry

    lax.fori_loop(0, tm, issue, 0)
    for j in range(2):
        pltpu.make_async_copy(ys_ref.at[pl.ds(0, tm)], y_ref.at[j], sem_ref.at[j]).wait()

    info = info_ref[...]
    w0 = info[:, R_W0:R_W0 + 1]
    w1 = info[:, R_W1:R_W1 + 1]
    xn = x_ref[...] + (w0 * y_ref[0] + w1 * y_ref[1])
    xo_ref[...] = xn
    ho_ref[...] = _rms(xn, g_ref[...]).astype(ho_ref.dtype)


def _combine(x, info, ys, pos_flat, next_g, h_dtype, tm=256):
    n, d = x.shape
    grid_spec = pltpu.PrefetchScalarGridSpec(
        num_scalar_prefetch=1,
        grid=(n // tm,),
        in_specs=[pl.BlockSpec((tm, d), lambda i, pos: (i, 0)),
                  pl.BlockSpec((tm, LANES), lambda i, pos: (i, 0)),
                  pl.BlockSpec((1, d), lambda i, pos: (0, 0)),
                  pl.BlockSpec(memory_space=pl.ANY)],
        out_specs=[pl.BlockSpec((tm, d), lambda i, pos: (i, 0)),
                   pl.BlockSpec((tm, d), lambda i, pos: (i, 0))],
        scratch_shapes=[pltpu.VMEM((2, tm, d), F32), pltpu.SemaphoreType.DMA((2,))],
    )
    return pl.pallas_call(
        functools.partial(_combine_body, tm=tm),
        grid_spec=grid_spec,
        out_shape=[jax.ShapeDtypeStruct((n, d), F32), jax.ShapeDtypeStruct((n, d), h_dtype)],
        compiler_params=_params(("arbitrary",), 32),
        name="moe_combine",
    )(pos_flat, x, info, next_g.reshape(1, d), ys)


def _moe(x, ln_g, router_w, w1, w3, w2, next_g, h_dtype):
    n, d = x.shape
    router_w_pad = jnp.pad(router_w, ((0, 0), (0, LANES - N_EXPERTS)))
    info, cnt = _router(x, ln_g, router_w_pad)

    counts = cnt[0, :N_EXPERTS].astype(jnp.int32)
    padded = ((counts + MOE_BLOCK - 1) // MOE_BLOCK) * MOE_BLOCK
    pend = jnp.cumsum(padded)
    pstart = pend - padded
    experts = info[:, R_E0:R_E1 + 1].astype(jnp.int32)
    ranks = info[:, R_RANK0:R_RANK1 + 1].astype(jnp.int32)
    pos_flat = (pstart[experts] + ranks).reshape(-1)
    n_blocks = -(-(2 * n) // MOE_BLOCK) + N_EXPERTS
    block_e = jnp.clip(jnp.searchsorted(pend, jnp.arange(n_blocks, dtype=jnp.int32) * MOE_BLOCK,
                                        side="right"), 0, N_EXPERTS - 1).astype(jnp.int32)
    n_used = (pend[-1:] // MOE_BLOCK).astype(jnp.int32)

    xs = _dispatch(x, ln_g, pos_flat, n_blocks * MOE_BLOCK)
    ys = _experts(xs, block_e, n_used, w1, w3, w2)
    return _combine(x, info, ys, pos_flat, next_g, h_dtype)


def _in_weights(w_in):
    o = 3 * FOX_W
    ff = w_in[:, o:o + FOX_HEADS]
    o += FOX_HEADS
    gqkv = w_in[:, o:o + REST_QKV]
    o += REST_QKV
    gab = w_in[:, o:o + 2 * GDN_HEADS]
    o += 2 * GDN_HEADS
    gz = w_in[:, o:o + GDN_W]
    fox = w_in[:, :3 * FOX_W]
    fox = jnp.concatenate([fox[:, :FOX_W] * (FOX_HEAD_DIM ** -0.5), fox[:, FOX_W:]], axis=1)
    pad = jnp.zeros((w_in.shape[0], LANES - G_ROWS), w_in.dtype)
    return jnp.concatenate([fox, gqkv, gz, ff, gab, pad], axis=1).astype(BF16)


def _gate_params(f_bias, dt_bias, a_log):
    row0 = jnp.zeros((LANES,), F32).at[G_FOX:G_FOX + FOX_HEADS].set(f_bias)
    row0 = row0.at[G_DEC:G_DEC + GDN_HEADS].set(dt_bias)
    row1 = jnp.zeros((LANES,), F32).at[G_DEC:G_DEC + GDN_HEADS].set(a_log)
    return jnp.zeros((8, LANES), F32).at[0].set(row0).at[1].set(row1)


def kernel(x, ln1_g, w_in, fox_f_bias, fox_norm_g, gdn_conv_w, gdn_a_log, gdn_dt_bias, gdn_norm_g,
           w_out, ln2_g, ffn_w1, ffn_w3, ffn_w2, router_w, exp_w1, exp_w3, exp_w2, final_g):
    b, t, d = x.shape
    n = b * t
    depth = w_in.shape[0]
    xr = x.reshape(n, d)
    h = _rmsnorm_rows(xr, ln1_g[0], BF16)
    for layer in range(depth):
        fox2, rest2 = _inproj(h, _in_weights(w_in[layer]))
        fox3 = fox2.reshape(b, t, 3 * FOX_W)
        rest3 = rest2.reshape(b, t, REST_W)
        gates3, gates_t = _gates(rest3, _gate_params(fox_f_bias[layer], gdn_dt_bias[layer],
                                                     gdn_a_log[layer]))
        fo = _fox(fox3, gates3, gates_t, fox_norm_g[layer])
        go = _gdn(rest3, gates3, gates_t, gdn_conv_w[layer], gdn_norm_g[layer])
        dense = layer % 2 == 0
        xr, h2 = _outproj(xr, fo.reshape(n, FOX_W), go.reshape(n, GDN_W),
                          w_out[layer].astype(BF16), ln2_g[layer], want_h=dense)
        last = layer == depth - 1
        next_g = final_g if last else ln1_g[layer + 1]
        j = layer // 2
        if dense:
            xr, h = _ffn(xr, h2, ffn_w1[j].astype(BF16), ffn_w3[j].astype(BF16),
                         ffn_w2[j].astype(BF16), next_g)
        else:
            xr, h = _moe(xr, ln2_g[layer], router_w[j], exp_w1[j].astype(BF16),
                         exp_w3[j].astype(BF16), exp_w2[j].astype(BF16), next_g,
                         F32 if last else BF16)
    return h.reshape(b, t, d)
```

```python
import functools

import jax
import jax.numpy as jnp
from jax import lax
from jax.experimental import pallas as pl
from jax.experimental.pallas import tpu as pltpu

F32 = jnp.float32
BF16 = jnp.bfloat16
HI = lax.Precision.HIGHEST

D_MODEL = 1024
FOX_HEADS = 8
FOX_HEAD_DIM = 64
FOX_W = FOX_HEADS * FOX_HEAD_DIM
GDN_HEADS = 4
GDN_DK = 128
GDN_DV = 128
GDN_W = GDN_HEADS * GDN_DK
CONV_K = 4
GDN_CHUNK = 64
N_EXPERTS = 8
MOE_BLOCK = 512
EPS = 1e-6

LANES = 128
NEG_BIG = -1e30
MIB = 1024 * 1024

REST_QKV = 3 * GDN_W
REST_Z = REST_QKV
REST_GATE = REST_QKV + GDN_W
REST_W = REST_GATE + LANES
GATE_BLK = REST_GATE // LANES
G_FOX = 0
G_DEC = FOX_HEADS
G_BETA = FOX_HEADS + GDN_HEADS
G_ROWS = FOX_HEADS + 2 * GDN_HEADS


def _dot(a, b, precision=None):
    return jnp.dot(a, b, preferred_element_type=F32, precision=precision)


def _dot_nt(a, b, precision=None):
    return lax.dot_general(a, b, (((1,), (1,)), ((), ())),
                           preferred_element_type=F32, precision=precision)


def _dot_tn(a, b, precision=None):
    return lax.dot_general(a, b, (((0,), (0,)), ((), ())),
                           preferred_element_type=F32, precision=precision)


def _params(semantics, vmem_mib):
    return pltpu.CompilerParams(dimension_semantics=semantics,
                                vmem_limit_bytes=vmem_mib * MIB)


def _rms(x, g):
    return x * lax.rsqrt(jnp.mean(x * x, axis=-1, keepdims=True) + EPS) * g


def _silu(x):
    return x * jax.nn.sigmoid(x)


def _resident(shape):
    nd = len(shape)
    return pl.BlockSpec(shape, lambda *_: (0,) * nd, pipeline_mode=pl.Buffered(1))


def _rms_body(x_ref, g_ref, o_ref):
    o_ref[...] = _rms(x_ref[...], g_ref[...]).astype(o_ref.dtype)


def _rmsnorm_rows(x, g, out_dtype, tm=1024):
    n, d = x.shape
    return pl.pallas_call(
        _rms_body,
        grid=(n // tm,),
        in_specs=[pl.BlockSpec((tm, d), lambda i: (i, 0)),
                  pl.BlockSpec((1, d), lambda i: (0, 0))],
        out_specs=pl.BlockSpec((tm, d), lambda i: (i, 0)),
        out_shape=jax.ShapeDtypeStruct((n, d), out_dtype),
        compiler_params=_params(("parallel",), 32),
        name="rmsnorm",
    )(x, g.reshape(1, d))


def _inproj_body(h_ref, w_ref, ofox_ref, orest_ref, *, col_chunk):
    h = h_ref[...]
    nf = ofox_ref.shape[1]
    nr = orest_ref.shape[1]
    for c in range(0, nf, col_chunk):
        ofox_ref[:, c:c + col_chunk] = _dot(h, w_ref[:, c:c + col_chunk]).astype(ofox_ref.dtype)
    for c in range(0, nr, col_chunk):
        e = min(c + col_chunk, nr)
        orest_ref[:, c:e] = _dot(h, w_ref[:, nf + c:nf + e])


def _inproj(h, w_all, tm=512):
    n, d = h.shape
    nf, nr = 3 * FOX_W, REST_W
    return pl.pallas_call(
        functools.partial(_inproj_body, col_chunk=512),
        grid=(n // tm,),
        in_specs=[pl.BlockSpec((tm, d), lambda i: (i, 0)),
                  _resident((d, nf + nr))],
        out_specs=[pl.BlockSpec((tm, nf), lambda i: (i, 0)),
                   pl.BlockSpec((tm, nr), lambda i: (i, 0))],
        out_shape=[jax.ShapeDtypeStruct((n, nf), BF16),
                   jax.ShapeDtypeStruct((n, nr), F32)],
        compiler_params=_params(("parallel",), 40),
        name="inproj",
    )(h, w_all)


def _gates_body(z_ref, p_ref, o_ref, ot_ref, ka_ref, carry_ref, *, tt):
    @pl.when(pl.program_id(1) == 0)
    def _():
        carry_ref[...] = jnp.zeros_like(carry_ref)

    z = z_ref[0] + p_ref[0:1, :]
    lane = lax.broadcasted_iota(jnp.int32, z.shape, 1)
    tail = jnp.log(1.0 + jnp.exp(-jnp.abs(z)))
    log_sig = jnp.minimum(z, 0.0) - tail
    softplus = jnp.maximum(z, 0.0) + tail
    decay = -jnp.exp(p_ref[1:2, :]) * softplus
    val = jnp.where(lane < G_DEC, log_sig, jnp.where(lane < G_BETA, decay, jax.nn.sigmoid(z)))

    ri = lax.broadcasted_iota(jnp.int32, (tt, tt), 0)
    ci = lax.broadcasted_iota(jnp.int32, (tt, tt), 1)
    tri = jnp.where(ci <= ri, 1.0, 0.0).astype(F32)
    blk = jnp.where((ci <= ri) & (ri // GDN_CHUNK == ci // GDN_CHUNK), 1.0, 0.0).astype(F32)
    full_cum = _dot(tri, val, HI) + carry_ref[...]
    chunk_cum = _dot(blk, val, HI)
    out = jnp.where(lane < G_DEC, full_cum, jnp.where(lane < G_BETA, chunk_cum, val))
    carry_ref[...] = full_cum[tt - 1:tt, :]
    o_ref[0] = out
    ot_ref[0] = out.T[:G_ROWS, :]

    neg_c = jnp.where(lane < G_DEC, -full_cum, 0.0)
    hi = neg_c.astype(BF16).astype(F32)
    mid = (neg_c - hi).astype(BF16).astype(F32)
    lo = (neg_c - hi - mid).astype(BF16).astype(F32)
    aug = hi + pltpu.roll(mid, FOX_HEADS, axis=1) + pltpu.roll(lo, 2 * FOX_HEADS, axis=1)
    ka_ref[0] = aug.astype(BF16)


def _gates(rest3, gate_params, tt=512):
    b, t, _ = rest3.shape
    return pl.pallas_call(
        functools.partial(_gates_body, tt=tt),
        grid=(b, t // tt),
        in_specs=[pl.BlockSpec((1, tt, LANES), lambda i, j: (i, j, GATE_BLK)),
                  pl.BlockSpec((8, LANES), lambda i, j: (0, 0))],
        out_specs=[pl.BlockSpec((1, tt, LANES), lambda i, j: (i, j, 0)),
                   pl.BlockSpec((1, G_ROWS, tt), lambda i, j: (i, 0, j)),
                   pl.BlockSpec((1, tt, LANES), lambda i, j: (i, j, 0))],
        out_shape=[jax.ShapeDtypeStruct((b, t, LANES), F32),
                   jax.ShapeDtypeStruct((b, G_ROWS, t), F32),
                   jax.ShapeDtypeStruct((b, t, LANES), BF16)],
        scratch_shapes=[pltpu.VMEM((1, LANES), F32)],
        compiler_params=_params(("parallel", "arbitrary"), 32),
        name="gates",
    )(rest3, gate_params)


def _fox_body(q_ref, k_ref, ka_ref, v_ref, c_ref, g_ref, o_ref, vt_ref, acc_ref, *, tq, q_sub):
    hp = pl.program_id(1)
    qi = pl.program_id(2)
    half = FOX_HEAD_DIM
    t_total = k_ref.shape[1]
    lane_row = lax.broadcasted_iota(jnp.int32, (1, LANES), 1)
    sub = lax.broadcasted_iota(jnp.int32, (LANES, tq), 0)

    @pl.when(qi == 0)
    def _():
        for c in range(t_total // tq):
            vt = v_ref[0, c * tq:(c + 1) * tq, :].astype(F32).T
            cols = slice(c * tq, (c + 1) * tq)
            vt_ref[0, :, cols] = jnp.where(sub < half, vt, jnp.where(sub == half, 1.0, 0.0)).astype(BF16)
            vt_ref[1, :, cols] = jnp.where(sub >= half, vt, jnp.where(sub == 0, 1.0, 0.0)).astype(BF16)

    q = q_ref[0]
    zero = jnp.zeros_like(q)
    qs = pl.multiple_of(qi * tq, tq)
    q_aug, cq = [], []
    for j in range(2):
        head = 2 * hp + j
        pick = (lane_row == head) | (lane_row == FOX_HEADS + head) | (lane_row == 2 * FOX_HEADS + head)
        ones = jnp.broadcast_to(jnp.where(pick, 1.0, 0.0).astype(BF16), (tq, LANES))
        qj = jnp.where(lane_row < half, q, zero) if j == 0 else jnp.where(lane_row < half, zero, q)
        q_aug.append(jnp.concatenate([qj, ones], axis=1).astype(F32).T.astype(BF16))
        cq.append(c_ref[0, 0, j:j + 1, pl.ds(qs, tq)])

    acc_ref[...] = jnp.zeros_like(acc_ref)

    def kv_step(ki, m, diagonal):
        ks = pl.multiple_of(ki * tq, tq)
        k_aug = jnp.concatenate([k_ref[0, pl.ds(ks, tq), :], ka_ref[0, pl.ds(ks, tq), :]], axis=1)
        chains = [(j, slice(c * q_sub, (c + 1) * q_sub)) for j in range(2) for c in range(tq // q_sub)]
        scores = []
        for j, cols in chains:
            s = _dot(k_aug, q_aug[j][:, cols])
            if diagonal:
                ri = lax.broadcasted_iota(jnp.int32, s.shape, 0)
                ci = lax.broadcasted_iota(jnp.int32, s.shape, 1) + cols.start
                s = jnp.where(ri <= ci, s, NEG_BIG)
            scores.append(s)
        m_out = []
        for (j, cols), s, m_old in zip(chains, scores, m):
            cqc = cq[j][:, cols]
            m_new = jnp.maximum(m_old, cqc + jnp.max(s, axis=0, keepdims=True))
            p = jnp.exp(s + (cqc - m_new)).astype(BF16)
            alpha = jnp.exp(m_old - m_new)
            acc_ref[j, :, cols] = alpha * acc_ref[j, :, cols] + _dot(vt_ref[j, :, pl.ds(ks, tq)], p)
            m_out.append(m_new)
        return tuple(m_out)

    m0 = jnp.full((1, q_sub), NEG_BIG, F32)
    m = lax.fori_loop(0, qi, lambda ki, m: kv_step(ki, m, False), (m0,) * (2 * (tq // q_sub)))
    kv_step(qi, m, True)

    a0 = acc_ref[0]
    a1 = acc_ref[1]
    ot = jnp.where(sub < half, a0 / a0[half:half + 1, :], a1 / a1[0:1, :])
    o = ot.T
    lo = lane_row < half
    sq = o * o
    ms0 = jnp.sum(jnp.where(lo, sq, 0.0), axis=1, keepdims=True) / half
    ms1 = jnp.sum(jnp.where(lo, 0.0, sq), axis=1, keepdims=True) / half
    inv = lax.rsqrt(jnp.where(lo, ms0, ms1) + EPS)
    o_ref[0] = (o * inv * g_ref[...]).astype(o_ref.dtype)


def _fox(fox3, gates_t, k_aug, norm_g, tq=512):
    b, t, _ = fox3.shape
    npair = FOX_HEADS // 2
    c4 = gates_t.reshape(b, G_ROWS // 2, 2, t)
    return pl.pallas_call(
        functools.partial(_fox_body, tq=tq, q_sub=512),
        grid=(b, npair, t // tq),
        in_specs=[pl.BlockSpec((1, tq, LANES), lambda i, p, j: (i, j, p)),
                  pl.BlockSpec((1, t, LANES), lambda i, p, j: (i, 0, npair + p)),
                  pl.BlockSpec((1, t, LANES), lambda i, p, j: (i, 0, 0)),
                  pl.BlockSpec((1, t, LANES), lambda i, p, j: (i, 0, 2 * npair + p)),
                  pl.BlockSpec((1, 1, 2, t), lambda i, p, j: (i, p, 0, 0)),
                  pl.BlockSpec((1, LANES), lambda i, p, j: (0, p))],
        out_specs=pl.BlockSpec((1, tq, LANES), lambda i, p, j: (i, j, p)),
        out_shape=jax.ShapeDtypeStruct((b, t, FOX_W), BF16),
        scratch_shapes=[pltpu.VMEM((2, LANES, t), BF16),
                        pltpu.VMEM((2, LANES, tq), F32)],
        compiler_params=_params(("parallel", "parallel", "arbitrary"), 48),
        name="fox_attention",
    )(fox3, fox3, k_aug, fox3, c4, norm_g.reshape(1, FOX_W))


SUPER = 256
GDN_HEADS_PER_STEP = 2
HALO = 8


def _split_bf16(x):
    hi = x.astype(BF16)
    return hi, (x - hi.astype(F32)).astype(BF16)


def _dot_split(a, b, dot=_dot):
    ah, al = a
    bh, bl = b
    return dot(ah, bh) + dot(ah, bl) + dot(al, bh)


def _gdn_body(xq_ref, xk_ref, xv_ref, z_ref, gc_ref, gr_ref, wq_ref, wk_ref, wv_ref, ng_ref,
              o_ref, s_ref, halo_ref, buf_ref, obuf_ref, *, tt):
    hps = GDN_HEADS_PER_STEP
    pair = pl.program_id(1)

    @pl.when(pl.program_id(2) == 0)
    def _():
        s_ref[...] = jnp.zeros_like(s_ref)
        halo_ref[...] = jnp.zeros_like(halo_ref)

    def conv_silu(x_ref, hh, idx, w_ref):
        lanes = slice(hh * LANES, (hh + 1) * LANES)
        slot = hh * 3 + idx
        x = x_ref[0, :, lanes]
        buf_ref[slot, 0:HALO, :] = halo_ref[slot]
        buf_ref[slot, HALO:HALO + tt, :] = x
        halo_ref[slot] = x[tt - HALO:tt, :]
        y = jnp.zeros_like(x)
        for j in range(CONV_K):
            off = HALO - (CONV_K - 1) + j
            y = y + w_ref[j:j + 1, lanes] * buf_ref[slot, off:off + tt, :]
        return _silu(y)

    gates = gc_ref[0]
    lane_t = lax.broadcasted_iota(jnp.int32, gates.shape, 1)
    ri = lax.broadcasted_iota(jnp.int32, (SUPER, SUPER), 0)
    ci = lax.broadcasted_iota(jnp.int32, (SUPER, SUPER), 1)
    same = (ri // GDN_CHUNK) == (ci // GDN_CHUNK)
    tril = same & (ci <= ri)
    strict = same & (ci < ri)
    eye = jnp.where(ri == ci, 1.0, 0.0).astype(F32)

    heads = []
    for hh in range(hps):
        head = hps * pair + hh
        q = conv_silu(xq_ref, hh, 0, wq_ref)
        k = conv_silu(xk_ref, hh, 1, wk_ref)
        v = conv_silu(xv_ref, hh, 2, wv_ref)
        q = q * lax.rsqrt(jnp.sum(q * q, axis=-1, keepdims=True) + EPS) * (GDN_DK ** -0.5)
        k = k * lax.rsqrt(jnp.sum(k * k, axis=-1, keepdims=True) + EPS)
        gcol = jnp.sum(jnp.where(lane_t == G_DEC + head, gates, 0.0), axis=1, keepdims=True)
        bcol = jnp.sum(jnp.where(lane_t == G_BETA + head, gates, 0.0), axis=1, keepdims=True)
        eg = jnp.exp(gcol)
        kb = k * bcol
        heads.append(dict(hh=hh, k=k, gcol=gcol, grow=gr_ref[0, hh], kb_split=_split_bf16(kb),
                          k_split=_split_bf16(k), k16=k.astype(BF16), q16=q.astype(BF16),
                          rhs16=jnp.concatenate([v * bcol, kb * eg], axis=1).astype(BF16),
                          qe16=(q * eg).astype(BF16)))

    blocks = []
    for hd in heads:
        for sc in range(tt // SUPER):
            rows = slice(sc * SUPER, (sc + 1) * SUPER)
            diff = hd["gcol"][rows] - hd["grow"][:, rows]
            decay = jnp.where(tril, jnp.exp(jnp.where(tril, diff, 0.0)), 0.0)
            gram = _dot_split(tuple(x[rows] for x in hd["kb_split"]),
                              tuple(x[rows] for x in hd["k_split"]), _dot_nt)
            a = jnp.where(strict, gram * decay, 0.0)
            blocks.append(dict(hd=hd, rows=rows, decay=decay, inv=eye - a, pw=_split_bf16(a)))

    for _ in range(5):
        for blk in blocks:
            blk["pw"] = _split_bf16(_dot_split(blk["pw"], blk["pw"]))
        for blk in blocks:
            blk["inv"] = blk["inv"] + _dot_split(_split_bf16(blk["inv"]), blk["pw"])

    for blk in blocks:
        hd, rows = blk["hd"], blk["rows"]
        inv_hi, inv_lo = _split_bf16(blk["inv"])
        blk["uw"] = _dot(inv_hi, hd["rhs16"][rows]) + _dot(inv_lo, hd["rhs16"][rows])
        blk["intra"] = jnp.where(tril, _dot_nt(hd["q16"][rows], hd["k16"][rows]) * blk["decay"],
                                 0.0).astype(BF16)

    for sc in range(tt // SUPER):
        for c in range(SUPER // GDN_CHUNK):
            lr = slice(c * GDN_CHUNK, (c + 1) * GDN_CHUNK)
            gr = slice(sc * SUPER + lr.start, sc * SUPER + lr.stop)
            for blk in blocks:
                if blk["rows"].start != sc * SUPER:
                    continue
                hd = blk["hd"]
                hh = hd["hh"]
                s = s_ref[hh]
                s16 = s.astype(BF16)
                v_new = blk["uw"][lr, :GDN_DV] - _dot(blk["uw"][lr, GDN_DV:].astype(BF16), s16)
                v16 = v_new.astype(BF16)
                obuf_ref[hh, gr, :] = _dot(hd["qe16"][gr], s16) + _dot(blk["intra"][lr, lr], v16)
                g_last = hd["gcol"][gr.stop - 1:gr.stop, :]
                k_dec = (hd["k"][gr] * jnp.exp(g_last - hd["gcol"][gr])).astype(BF16)
                s_ref[hh] = s * jnp.exp(g_last) + _dot_tn(k_dec, v16)

    for hh in range(hps):
        lanes = slice(hh * LANES, (hh + 1) * LANES)
        o = _rms(obuf_ref[hh], ng_ref[...]) * _silu(z_ref[0, :, lanes])
        o_ref[0, :, lanes] = o.astype(o_ref.dtype)


def _gdn(rest3, gates3, gates_t, conv_w, norm_g, tt=512):
    b, t, _ = rest3.shape
    hps = GDN_HEADS_PER_STEP
    npair = GDN_HEADS // hps
    wide = hps * LANES
    gr4 = gates_t.reshape(b, G_ROWS, 1, t)
    x_spec = lambda off: pl.BlockSpec((1, tt, wide), lambda i, p, j: (i, j, off + p))
    w_spec = lambda off: pl.BlockSpec((CONV_K, wide), lambda i, p, j: (0, off + p))
    return pl.pallas_call(
        functools.partial(_gdn_body, tt=tt),
        grid=(b, npair, t // tt),
        in_specs=[x_spec(0), x_spec(npair), x_spec(2 * npair), x_spec(3 * npair),
                  pl.BlockSpec((1, tt, LANES), lambda i, p, j: (i, j, 0)),
                  pl.BlockSpec((1, hps, 1, tt), lambda i, p, j: (i, G_DEC // hps + p, 0, j)),
                  w_spec(0), w_spec(npair), w_spec(2 * npair),
                  pl.BlockSpec((1, LANES), lambda i, p, j: (0, 0))],
        out_specs=pl.BlockSpec((1, tt, wide), lambda i, p, j: (i, j, p)),
        out_shape=jax.ShapeDtypeStruct((b, t, GDN_W), BF16),
        scratch_shapes=[pltpu.VMEM((hps, GDN_DK, GDN_DV), F32),
                        pltpu.VMEM((hps * 3, HALO, LANES), F32),
                        pltpu.VMEM((hps * 3, tt + HALO, LANES), F32),
                        pltpu.VMEM((hps, tt, LANES), F32)],
        compiler_params=_params(("parallel", "parallel", "arbitrary"), 48),
        name="gated_delta",
    )(rest3, rest3, rest3, rest3, gates3, gr4, conv_w, conv_w, conv_w, norm_g.reshape(1, GDN_DV))


def _outproj_body(x_ref, fo_ref, go_ref, w_ref, g_ref, xo_ref, *maybe_h_ref):
    y = _dot(fo_ref[...], w_ref[0:FOX_W, :]) + _dot(go_ref[...], w_ref[FOX_W:FOX_W + GDN_W, :])
    xn = x_ref[...] + y
    xo_ref[...] = xn
    if maybe_h_ref:
        maybe_h_ref[0][...] = _rms(xn, g_ref[...]).astype(BF16)


def _outproj(x, fo, go, w_out, ln_g, want_h, tm=512):
    n, d = x.shape
    row = lambda w: pl.BlockSpec((tm, w), lambda i: (i, 0))
    out_specs = [row(d)]
    out_shape = [jax.ShapeDtypeStruct((n, d), F32)]
    if want_h:
        out_specs.append(row(d))
        out_shape.append(jax.ShapeDtypeStruct((n, d), BF16))
    res = pl.pallas_call(
        _outproj_body,
        grid=(n // tm,),
        in_specs=[row(d), row(FOX_W), row(GDN_W), _resident((FOX_W + GDN_W, d)),
                  pl.BlockSpec((1, d), lambda i: (0, 0))],
        out_specs=out_specs,
        out_shape=out_shape,
        compiler_params=_params(("parallel",), 32),
        name="outproj",
    )(x, fo, go, w_out, ln_g.reshape(1, d))
    return res if want_h else (res[0], None)


def _ffn_body(x_ref, h_ref, w1_ref, w3_ref, w2_ref, g_ref, xo_ref, ho_ref, acc_ref, *, f_chunk):
    h = h_ref[...]
    d_ff = w1_ref.shape[1]
    for i, c in enumerate(range(0, d_ff, f_chunk)):
        a = _dot(h, w1_ref[:, c:c + f_chunk])
        b = _dot(h, w3_ref[:, c:c + f_chunk])
        y = _dot((_silu(a) * b).astype(BF16), w2_ref[c:c + f_chunk, :])
        if i == 0:
            acc_ref[...] = y
        else:
            acc_ref[...] += y
    xn = x_ref[...] + acc_ref[...]
    xo_ref[...] = xn
    ho_ref[...] = _rms(xn, g_ref[...]).astype(ho_ref.dtype)


def _ffn(x, h, w1, w3, w2, next_g, tm=512, f_chunk=256):
    n, d = x.shape
    d_ff = w1.shape[1]
    row = pl.BlockSpec((tm, d), lambda i: (i, 0))
    return pl.pallas_call(
        functools.partial(_ffn_body, f_chunk=f_chunk),
        grid=(n // tm,),
        in_specs=[row, row, _resident((d, d_ff)), _resident((d, d_ff)), _resident((d_ff, d)),
                  pl.BlockSpec((1, d), lambda i: (0, 0))],
        out_specs=[row, row],
        out_shape=[jax.ShapeDtypeStruct((n, d), F32), jax.ShapeDtypeStruct((n, d), BF16)],
        scratch_shapes=[pltpu.VMEM((tm, d), F32)],
        compiler_params=_params(("parallel",), 52),
        name="dense_swiglu",
    )(x, h, w1, w3, w2, next_g.reshape(1, d))


R_E0, R_E1, R_W0, R_W1, R_RANK0, R_RANK1 = 0, 1, 2, 3, 4, 5


def _router_body(x_ref, g_ref, w_ref, info_ref, cnt_ref, carry_ref, *, tm):
    @pl.when(pl.program_id(0) == 0)
    def _():
        carry_ref[...] = jnp.zeros_like(carry_ref)

    h = _rms(x_ref[...], g_ref[...])
    logits = _dot(h, w_ref[...], HI)
    lane = lax.broadcasted_iota(jnp.int32, logits.shape, 1)
    valid = lane < N_EXPERTS
    lm = jnp.where(valid, logits, NEG_BIG)
    ex = jnp.exp(lm - jnp.max(lm, axis=1, keepdims=True))
    probs = jnp.where(valid, ex / jnp.sum(ex, axis=1, keepdims=True), -1.0)
    p0 = jnp.max(probs, axis=1, keepdims=True)
    e0 = jnp.min(jnp.where(probs == p0, lane, LANES), axis=1, keepdims=True)
    rest = jnp.where(lane == e0, -1.0, probs)
    p1 = jnp.max(rest, axis=1, keepdims=True)
    e1 = jnp.min(jnp.where(rest == p1, lane, LANES), axis=1, keepdims=True)
    denom = p0 + p1

    picked = jnp.where((lane == e0) | (lane == e1), 1.0, 0.0).astype(F32)
    ri = lax.broadcasted_iota(jnp.int32, (tm, tm), 0)
    ci = lax.broadcasted_iota(jnp.int32, (tm, tm), 1)
    before = jnp.where(ci < ri, 1.0, 0.0).astype(BF16)
    earlier = _dot(before, picked.astype(BF16)) + carry_ref[...]
    rank0 = jnp.sum(jnp.where(lane == e0, earlier, 0.0), axis=1, keepdims=True)
    rank1 = jnp.sum(jnp.where(lane == e1, earlier, 0.0), axis=1, keepdims=True)
    carry_ref[...] += jnp.sum(picked, axis=0, keepdims=True)

    info = jnp.zeros(logits.shape, F32)
    for col, val in ((R_E0, e0.astype(F32)), (R_E1, e1.astype(F32)), (R_W0, p0 / denom),
                     (R_W1, p1 / denom), (R_RANK0, rank0), (R_RANK1, rank1)):
        info = jnp.where(lane == col, val, info)
    info_ref[...] = info
    cnt_ref[...] = jnp.broadcast_to(carry_ref[...], cnt_ref.shape)


def _router(x, ln_g, router_w_pad, tm=512):
    n, d = x.shape
    return pl.pallas_call(
        functools.partial(_router_body, tm=tm),
        grid=(n // tm,),
        in_specs=[pl.BlockSpec((tm, d), lambda i: (i, 0)),
                  pl.BlockSpec((1, d), lambda i: (0, 0)),
                  pl.BlockSpec((d, LANES), lambda i: (0, 0))],
        out_specs=[pl.BlockSpec((tm, LANES), lambda i: (i, 0)),
                   pl.BlockSpec((8, LANES), lambda i: (0, 0))],
        out_shape=[jax.ShapeDtypeStruct((n, LANES), F32),
                   jax.ShapeDtypeStruct((8, LANES), F32)],
        scratch_shapes=[pltpu.VMEM((1, LANES), F32)],
        compiler_params=_params(("arbitrary",), 32),
        name="moe_router",
    )(x, ln_g.reshape(1, d), router_w_pad)


def _row_copy(src_ref, src_row, dst_ref, dst_row, sem):
    return pltpu.make_async_copy(src_ref.at[pl.ds(src_row, 1)], dst_ref.at[pl.ds(dst_row, 1)], sem)


def _dispatch_body(pos_ref, x_ref, g_ref, xs_in_ref, xs_ref, buf_ref, sem_ref, *, tm):
    del xs_in_ref
    i = pl.program_id(0)
    slot = i % 2
    buf_ref[slot] = _rms(x_ref[...], g_ref[...])

    def issue(r, carry):
        base = 2 * (i * tm + r)
        for j in range(2):
            _row_copy(buf_ref.at[slot], r, xs_ref, pos_ref[base + j], sem_ref.at[slot]).start()
        return carry

    lax.fori_loop(0, tm, issue, 0)

    def wait_slot(s):
        for _ in range(2):
            pltpu.make_async_copy(buf_ref.at[s], xs_ref.at[pl.ds(0, tm)], sem_ref.at[s]).wait()

    @pl.when(i > 0)
    def _():
        wait_slot(1 - slot)

    @pl.when(i == pl.num_programs(0) - 1)
    def _():
        wait_slot(slot)


def _dispatch(x, ln_g, pos_flat, n_slots, tm=256):
    n, d = x.shape
    grid_spec = pltpu.PrefetchScalarGridSpec(
        num_scalar_prefetch=1,
        grid=(n // tm,),
        in_specs=[pl.BlockSpec((tm, d), lambda i, pos: (i, 0)),
                  pl.BlockSpec((1, d), lambda i, pos: (0, 0)),
                  pl.BlockSpec(memory_space=pl.ANY)],
        out_specs=pl.BlockSpec(memory_space=pl.ANY),
        scratch_shapes=[pltpu.VMEM((2, tm, d), F32), pltpu.SemaphoreType.DMA((2,))],
    )
    return pl.pallas_call(
        functools.partial(_dispatch_body, tm=tm),
        grid_spec=grid_spec,
        out_shape=jax.ShapeDtypeStruct((n_slots, d), F32),
        input_output_aliases={3: 0},
        compiler_params=_params(("arbitrary",), 32),
        name="moe_dispatch",
    )(pos_flat, x, ln_g.reshape(1, d), jnp.zeros((n_slots, d), F32))


def _experts_body(be_ref, nu_ref, xs_ref, w1_ref, w3_ref, w2_ref, o_ref, xb_ref):
    b = pl.program_id(0)
    f = pl.program_id(1)
    used = b < nu_ref[0]

    @pl.when(used & (f == 0))
    def _():
        xb_ref[...] = xs_ref[...].astype(BF16)

    @pl.when(used)
    def _():
        xb = xb_ref[...]
        a = _dot(xb, w1_ref[0])
        g = _dot(xb, w3_ref[0])
        y = _dot((_silu(a) * g).astype(BF16), w2_ref[0])

        @pl.when(f == 0)
        def _():
            o_ref[...] = y

        @pl.when(f > 0)
        def _():
            o_ref[...] += y

    @pl.when(jnp.logical_not(used) & (f == 0))
    def _():
        o_ref[...] = jnp.zeros_like(o_ref)


def _experts(xs, block_e, n_used, w1, w3, w2, f_chunk=896):
    p, d = xs.shape
    d_ff = w1.shape[2]
    nb = p // MOE_BLOCK
    grid_spec = pltpu.PrefetchScalarGridSpec(
        num_scalar_prefetch=2,
        grid=(nb, d_ff // f_chunk),
        in_specs=[pl.BlockSpec((MOE_BLOCK, d), lambda b, f, be, nu: (b, 0)),
                  pl.BlockSpec((1, d, f_chunk), lambda b, f, be, nu: (be[b], 0, f)),
                  pl.BlockSpec((1, d, f_chunk), lambda b, f, be, nu: (be[b], 0, f)),
                  pl.BlockSpec((1, f_chunk, d), lambda b, f, be, nu: (be[b], f, 0))],
        out_specs=pl.BlockSpec((MOE_BLOCK, d), lambda b, f, be, nu: (b, 0)),
        scratch_shapes=[pltpu.VMEM((MOE_BLOCK, d), BF16)],
    )
    return pl.pallas_call(
        _experts_body,
        grid_spec=grid_spec,
        out_shape=jax.ShapeDtypeStruct((p, d), F32),
        compiler_params=_params(("arbitrary", "arbitrary"), 48),
        name="moe_experts",
    )(block_e, n_used, xs, w1, w3, w2)


def _combine_body(pos_ref, x_ref, info_ref, g_ref, ys_ref, xo_ref, ho_ref, y_ref, sem_ref, *, tm):
    i = pl.program_id(0)

    def issue(r, carry):
        base = 2 * (i * tm + r)
        for j in range(2):
            _row_copy(ys_ref, pos_ref[base + j], y_ref.at[j], r, sem_ref.at[j]).start()
        return carry

    lax.fori_loop(0, tm, issue, 0)
    for j in range(2):
        pltpu.make_async_copy(ys_ref.at[pl.ds(0, tm)], y_ref.at[j], sem_ref.at[j]).wait()

    info = info_ref[...]
    w0 = info[:, R_W0:R_W0 + 1]
    w1 = info[:, R_W1:R_W1 + 1]
    xn = x_ref[...] + (w0 * y_ref[0] + w1 * y_ref[1])
    xo_ref[...] = xn
    ho_ref[...] = _rms(xn, g_ref[...]).astype(ho_ref.dtype)


def _combine(x, info, ys, pos_flat, next_g, h_dtype, tm=256):
    n, d = x.shape
    grid_spec = pltpu.PrefetchScalarGridSpec(
        num_scalar_prefetch=1,
        grid=(n // tm,),
        in_specs=[pl.BlockSpec((tm, d), lambda i, pos: (i, 0)),
                  pl.BlockSpec((tm, LANES), lambda i, pos: (i, 0)),
                  pl.BlockSpec((1, d), lambda i, pos: (0, 0)),
                  pl.BlockSpec(memory_space=pl.ANY)],
        out_specs=[pl.BlockSpec((tm, d), lambda i, pos: (i, 0)),
                   pl.BlockSpec((tm, d), lambda i, pos: (i, 0))],
        scratch_shapes=[pltpu.VMEM((2, tm, d), F32), pltpu.SemaphoreType.DMA((2,))],
    )
    return pl.pallas_call(
        functools.partial(_combine_body, tm=tm),
        grid_spec=grid_spec,
        out_shape=[jax.ShapeDtypeStruct((n, d), F32), jax.ShapeDtypeStruct((n, d), h_dtype)],
        compiler_params=_params(("arbitrary",), 32),
        name="moe_combine",
    )(pos_flat, x, info, next_g.reshape(1, d), ys)


def _moe(x, ln_g, router_w, w1, w3, w2, next_g, h_dtype):
    n, d = x.shape
    router_w_pad = jnp.pad(router_w, ((0, 0), (0, LANES - N_EXPERTS)))
    info, cnt = _router(x, ln_g, router_w_pad)

    counts = cnt[0, :N_EXPERTS].astype(jnp.int32)
    padded = ((counts + MOE_BLOCK - 1) // MOE_BLOCK) * MOE_BLOCK
    pend = jnp.cumsum(padded)
    pstart = pend - padded
    experts = info[:, R_E0:R_E1 + 1].astype(jnp.int32)
    ranks = info[:, R_RANK0:R_RANK1 + 1].astype(jnp.int32)
    pos_flat = (pstart[experts] + ranks).reshape(-1)
    n_blocks = -(-(2 * n) // MOE_BLOCK) + N_EXPERTS
    block_e = jnp.clip(jnp.searchsorted(pend, jnp.arange(n_blocks, dtype=jnp.int32) * MOE_BLOCK,
                                        side="right"), 0, N_EXPERTS - 1).astype(jnp.int32)
    n_used = (pend[-1:] // MOE_BLOCK).astype(jnp.int32)

    xs = _dispatch(x, ln_g, pos_flat, n_blocks * MOE_BLOCK)
    ys = _experts(xs, block_e, n_used, w1, w3, w2)
    return _combine(x, info, ys, pos_flat, next_g, h_dtype)


def _in_weights(w_in):
    o = 3 * FOX_W
    ff = w_in[:, o:o + FOX_HEADS]
    o += FOX_HEADS
    gqkv = w_in[:, o:o + REST_QKV]
    o += REST_QKV
    gab = w_in[:, o:o + 2 * GDN_HEADS]
    o += 2 * GDN_HEADS
    gz = w_in[:, o:o + GDN_W]
    fox = w_in[:, :3 * FOX_W]
    fox = jnp.concatenate([fox[:, :FOX_W] * (FOX_HEAD_DIM ** -0.5), fox[:, FOX_W:]], axis=1)
    pad = jnp.zeros((w_in.shape[0], LANES - G_ROWS), w_in.dtype)
    return jnp.concatenate([fox, gqkv, gz, ff, gab, pad], axis=1).astype(BF16)


def _gate_params(f_bias, dt_bias, a_log):
    row0 = jnp.zeros((LANES,), F32).at[G_FOX:G_FOX + FOX_HEADS].set(f_bias)
    row0 = row0.at[G_DEC:G_DEC + GDN_HEADS].set(dt_bias)
    row1 = jnp.zeros((LANES,), F32).at[G_DEC:G_DEC + GDN_HEADS].set(a_log)
    return jnp.zeros((8, LANES), F32).at[0].set(row0).at[1].set(row1)


def kernel(x, ln1_g, w_in, fox_f_bias, fox_norm_g, gdn_conv_w, gdn_a_log, gdn_dt_bias, gdn_norm_g,
           w_out, ln2_g, ffn_w1, ffn_w3, ffn_w2, router_w, exp_w1, exp_w3, exp_w2, final_g):
    b, t, d = x.shape
    n = b * t
    depth = w_in.shape[0]
    xr = x.reshape(n, d)
    h = _rmsnorm_rows(xr, ln1_g[0], BF16)
    for layer in range(depth):
        fox2, rest2 = _inproj(h, _in_weights(w_in[layer]))
        fox3 = fox2.reshape(b, t, 3 * FOX_W)
        rest3 = rest2.reshape(b, t, REST_W)
        gates3, gates_t, k_aug = _gates(rest3, _gate_params(fox_f_bias[layer], gdn_dt_bias[layer],
                                                            gdn_a_log[layer]))
        fo = _fox(fox3, gates_t, k_aug, fox_norm_g[layer])
        go = _gdn(rest3, gates3, gates_t, gdn_conv_w[layer], gdn_norm_g[layer])
        dense = layer % 2 == 0
        xr, h2 = _outproj(xr, fo.reshape(n, FOX_W), go.reshape(n, GDN_W),
                          w_out[layer].astype(BF16), ln2_g[layer], want_h=dense)
        last = layer == depth - 1
        next_g = final_g if last else ln1_g[layer + 1]
        j = layer // 2
        if dense:
            xr, h = _ffn(xr, h2, ffn_w1[j].astype(BF16), ffn_w3[j].astype(BF16),
                         ffn_w2[j].astype(BF16), next_g)
        else:
            xr, h = _moe(xr, ln2_g[layer], router_w[j], exp_w1[j].astype(BF16),
                         exp_w3[j].astype(BF16), exp_w2[j].astype(BF16), next_g,
                         F32 if last else BF16)
    return h.reshape(b, t, d)
```

```python
import functools

import jax
import jax.numpy as jnp
from jax import lax
from jax.experimental import pallas as pl
from jax.experimental.pallas import tpu as pltpu

F32 = jnp.float32
BF16 = jnp.bfloat16
HI = lax.Precision.HIGHEST

D_MODEL = 1024
FOX_HEADS = 8
FOX_HEAD_DIM = 64
FOX_W = FOX_HEADS * FOX_HEAD_DIM
GDN_HEADS = 4
GDN_DK = 128
GDN_DV = 128
GDN_W = GDN_HEADS * GDN_DK
CONV_K = 4
GDN_CHUNK = 64
N_EXPERTS = 8
MOE_BLOCK = 512
EPS = 1e-6

LANES = 128
NEG_BIG = -1e30
LOG2E = 1.4426950408889634
MIB = 1024 * 1024

REST_QKV = 3 * GDN_W
REST_Z = REST_QKV
REST_GATE = REST_QKV + GDN_W
REST_W = REST_GATE + LANES
GATE_BLK = REST_GATE // LANES
G_FOX = 0
G_DEC = FOX_HEADS
G_BETA = FOX_HEADS + GDN_HEADS
G_ROWS = FOX_HEADS + 2 * GDN_HEADS


def _dot(a, b, precision=None):
    return jnp.dot(a, b, preferred_element_type=F32, precision=precision)


def _dot_nt(a, b, precision=None):
    return lax.dot_general(a, b, (((1,), (1,)), ((), ())),
                           preferred_element_type=F32, precision=precision)


def _dot_tn(a, b, precision=None):
    return lax.dot_general(a, b, (((0,), (0,)), ((), ())),
                           preferred_element_type=F32, precision=precision)


def _params(semantics, vmem_mib):
    return pltpu.CompilerParams(dimension_semantics=semantics,
                                vmem_limit_bytes=vmem_mib * MIB)


def _rms(x, g):
    return x * lax.rsqrt(jnp.mean(x * x, axis=-1, keepdims=True) + EPS) * g


def _silu(x):
    return x * jax.nn.sigmoid(x)


def _resident(shape):
    nd = len(shape)
    return pl.BlockSpec(shape, lambda *_: (0,) * nd, pipeline_mode=pl.Buffered(1))


def _rms_body(x_ref, g_ref, o_ref):
    o_ref[...] = _rms(x_ref[...], g_ref[...]).astype(o_ref.dtype)


def _rmsnorm_rows(x, g, out_dtype, tm=1024):
    n, d = x.shape
    return pl.pallas_call(
        _rms_body,
        grid=(n // tm,),
        in_specs=[pl.BlockSpec((tm, d), lambda i: (i, 0)),
                  pl.BlockSpec((1, d), lambda i: (0, 0))],
        out_specs=pl.BlockSpec((tm, d), lambda i: (i, 0)),
        out_shape=jax.ShapeDtypeStruct((n, d), out_dtype),
        compiler_params=_params(("parallel",), 32),
        name="rmsnorm",
    )(x, g.reshape(1, d))


def _inproj_body(h_ref, w_ref, ofox_ref, orest_ref, *, col_chunk):
    h = h_ref[...]
    nf = ofox_ref.shape[1]
    nr = orest_ref.shape[1]
    for c in range(0, nf, col_chunk):
        ofox_ref[:, c:c + col_chunk] = _dot(h, w_ref[:, c:c + col_chunk]).astype(ofox_ref.dtype)
    for c in range(0, nr, col_chunk):
        e = min(c + col_chunk, nr)
        orest_ref[:, c:e] = _dot(h, w_ref[:, nf + c:nf + e])


def _inproj(h, w_all, tm=512):
    n, d = h.shape
    nf, nr = 3 * FOX_W, REST_W
    return pl.pallas_call(
        functools.partial(_inproj_body, col_chunk=512),
        grid=(n // tm,),
        in_specs=[pl.BlockSpec((tm, d), lambda i: (i, 0)),
                  _resident((d, nf + nr))],
        out_specs=[pl.BlockSpec((tm, nf), lambda i: (i, 0)),
                   pl.BlockSpec((tm, nr), lambda i: (i, 0))],
        out_shape=[jax.ShapeDtypeStruct((n, nf), BF16),
                   jax.ShapeDtypeStruct((n, nr), F32)],
        compiler_params=_params(("parallel",), 40),
        name="inproj",
    )(h, w_all)


def _gates_body(z_ref, p_ref, o_ref, ot_ref, ka_ref, carry_ref, *, tt):
    @pl.when(pl.program_id(1) == 0)
    def _():
        carry_ref[...] = jnp.zeros_like(carry_ref)

    z = z_ref[0] + p_ref[0:1, :]
    lane = lax.broadcasted_iota(jnp.int32, z.shape, 1)
    tail = jnp.log(1.0 + jnp.exp(-jnp.abs(z)))
    log_sig = jnp.minimum(z, 0.0) - tail
    softplus = jnp.maximum(z, 0.0) + tail
    decay = -jnp.exp(p_ref[1:2, :]) * softplus
    val = jnp.where(lane < G_DEC, log_sig, jnp.where(lane < G_BETA, decay, jax.nn.sigmoid(z)))

    ri = lax.broadcasted_iota(jnp.int32, (tt, tt), 0)
    ci = lax.broadcasted_iota(jnp.int32, (tt, tt), 1)
    tri = jnp.where(ci <= ri, 1.0, 0.0).astype(F32)
    blk = jnp.where((ci <= ri) & (ri // GDN_CHUNK == ci // GDN_CHUNK), 1.0, 0.0).astype(F32)
    full_cum = _dot(tri, val, HI) + carry_ref[...]
    chunk_cum = _dot(blk, val, HI)
    out = jnp.where(lane < G_DEC, full_cum, jnp.where(lane < G_BETA, chunk_cum, val))
    carry_ref[...] = full_cum[tt - 1:tt, :]
    o_ref[0] = out
    ot_ref[0] = out.T[:G_ROWS, :]

    neg_c = jnp.where(lane < G_DEC, -LOG2E * full_cum, 0.0)
    hi = neg_c.astype(BF16).astype(F32)
    mid = (neg_c - hi).astype(BF16).astype(F32)
    lo = (neg_c - hi - mid).astype(BF16).astype(F32)
    aug = hi + pltpu.roll(mid, FOX_HEADS, axis=1) + pltpu.roll(lo, 2 * FOX_HEADS, axis=1)
    ka_ref[0] = aug.astype(BF16)


def _gates(rest3, gate_params, tt=512):
    b, t, _ = rest3.shape
    return pl.pallas_call(
        functools.partial(_gates_body, tt=tt),
        grid=(b, t // tt),
        in_specs=[pl.BlockSpec((1, tt, LANES), lambda i, j: (i, j, GATE_BLK)),
                  pl.BlockSpec((8, LANES), lambda i, j: (0, 0))],
        out_specs=[pl.BlockSpec((1, tt, LANES), lambda i, j: (i, j, 0)),
                   pl.BlockSpec((1, G_ROWS, tt), lambda i, j: (i, 0, j)),
                   pl.BlockSpec((1, tt, LANES), lambda i, j: (i, j, 0))],
        out_shape=[jax.ShapeDtypeStruct((b, t, LANES), F32),
                   jax.ShapeDtypeStruct((b, G_ROWS, t), F32),
                   jax.ShapeDtypeStruct((b, t, LANES), BF16)],
        scratch_shapes=[pltpu.VMEM((1, LANES), F32)],
        compiler_params=_params(("parallel", "arbitrary"), 32),
        name="gates",
    )(rest3, gate_params)


def _fox_body(q_ref, k_ref, ka_ref, v_ref, c_ref, g_ref, o_ref, vt_ref, acc_ref, s_ref, cm_ref, m_ref,
              *, tq):
    hp = pl.program_id(1)
    qi = pl.program_id(2)
    half = FOX_HEAD_DIM
    t_total = k_ref.shape[1]
    lane_row = lax.broadcasted_iota(jnp.int32, (1, LANES), 1)
    sub = lax.broadcasted_iota(jnp.int32, (LANES, tq), 0)

    @pl.when(qi == 0)
    def _():
        for c in range(t_total // tq):
            vt = v_ref[0, c * tq:(c + 1) * tq, :].astype(F32).T
            cols = slice(c * tq, (c + 1) * tq)
            vt_ref[0, :, cols] = jnp.where(sub < half, vt, jnp.where(sub == half, 1.0, 0.0)).astype(BF16)
            vt_ref[1, :, cols] = jnp.where(sub >= half, vt, jnp.where(sub == 0, 1.0, 0.0)).astype(BF16)

    q = q_ref[0]
    zero = jnp.zeros_like(q)
    qs = pl.multiple_of(qi * tq, tq)
    q_aug, cq = [], []
    for j in range(2):
        head = 2 * hp + j
        pick = (lane_row == head) | (lane_row == FOX_HEADS + head) | (lane_row == 2 * FOX_HEADS + head)
        ones = jnp.broadcast_to(jnp.where(pick, 1.0, 0.0).astype(BF16), (tq, LANES))
        qj = jnp.where(lane_row < half, q, zero) if j == 0 else jnp.where(lane_row < half, zero, q)
        q_aug.append(jnp.concatenate([qj, ones], axis=1).astype(F32).T.astype(BF16))
        cq.append(c_ref[0, 0, j:j + 1, pl.ds(qs, tq)] * LOG2E)

    acc_ref[...] = jnp.zeros_like(acc_ref)
    m_ref[...] = jnp.full(m_ref.shape, NEG_BIG, F32)

    def scores(ki, j, diagonal):
        ks = pl.multiple_of(ki * tq, tq)
        k_aug = jnp.concatenate([k_ref[0, pl.ds(ks, tq), :], ka_ref[0, pl.ds(ks, tq), :]], axis=1)
        s = _dot(k_aug, q_aug[j])
        if diagonal:
            ri = lax.broadcasted_iota(jnp.int32, s.shape, 0)
            ci = lax.broadcasted_iota(jnp.int32, s.shape, 1)
            s = jnp.where(ri <= ci, s, NEG_BIG)
        return s

    def stash(j, s):
        s_ref[j] = s
        cm_ref[j] = jnp.max(s, axis=0, keepdims=True)

    def consume(ki, j):
        ks = pl.multiple_of(ki * tq, tq)
        m_old = m_ref[j]
        m_new = jnp.maximum(m_old, cq[j] + cm_ref[j])
        p = jnp.exp2(s_ref[j] + (cq[j] - m_new)).astype(BF16)
        acc_ref[j] = jnp.exp2(m_old - m_new) * acc_ref[j] + _dot(vt_ref[j, :, pl.ds(ks, tq)], p)
        m_ref[j] = m_new

    def advance(ki_next, diagonal):
        for j in range(2):
            s_next = scores(ki_next, j, diagonal)
            consume(ki_next - 1, j)
            stash(j, s_next)

    @pl.when(qi == 0)
    def _():
        for j in range(2):
            stash(j, scores(0, j, True))

    @pl.when(qi > 0)
    def _():
        for j in range(2):
            stash(j, scores(0, j, False))

    def steady(ki, carry):
        advance(ki + 1, False)
        return carry

    lax.fori_loop(0, qi - 1, steady, 0)

    @pl.when(qi > 0)
    def _():
        advance(qi, True)

    for j in range(2):
        consume(qi, j)

    a0 = acc_ref[0]
    a1 = acc_ref[1]
    ot = jnp.where(sub < half, a0 / a0[half:half + 1, :], a1 / a1[0:1, :])
    o = ot.T
    lo = lane_row < half
    sq = o * o
    ms0 = jnp.sum(jnp.where(lo, sq, 0.0), axis=1, keepdims=True) / half
    ms1 = jnp.sum(jnp.where(lo, 0.0, sq), axis=1, keepdims=True) / half
    inv = lax.rsqrt(jnp.where(lo, ms0, ms1) + EPS)
    o_ref[0] = (o * inv * g_ref[...]).astype(o_ref.dtype)


def _fox(fox3, gates_t, k_aug, norm_g, tq=512):
    b, t, _ = fox3.shape
    npair = FOX_HEADS // 2
    c4 = gates_t.reshape(b, G_ROWS // 2, 2, t)
    return pl.pallas_call(
        functools.partial(_fox_body, tq=tq),
        grid=(b, npair, t // tq),
        in_specs=[pl.BlockSpec((1, tq, LANES), lambda i, p, j: (i, j, p)),
                  pl.BlockSpec((1, t, LANES), lambda i, p, j: (i, 0, npair + p)),
                  pl.BlockSpec((1, t, LANES), lambda i, p, j: (i, 0, 0)),
                  pl.BlockSpec((1, t, LANES), lambda i, p, j: (i, 0, 2 * npair + p)),
                  pl.BlockSpec((1, 1, 2, t), lambda i, p, j: (i, p, 0, 0)),
                  pl.BlockSpec((1, LANES), lambda i, p, j: (0, p))],
        out_specs=pl.BlockSpec((1, tq, LANES), lambda i, p, j: (i, j, p)),
        out_shape=jax.ShapeDtypeStruct((b, t, FOX_W), BF16),
        scratch_shapes=[pltpu.VMEM((2, LANES, t), BF16),
                        pltpu.VMEM((2, LANES, tq), F32),
                        pltpu.VMEM((2, tq, tq), F32),
                        pltpu.VMEM((2, 1, tq), F32),
                        pltpu.VMEM((2, 1, tq), F32)],
        compiler_params=_params(("parallel", "parallel", "arbitrary"), 48),
        name="fox_attention",
    )(fox3, fox3, k_aug, fox3, c4, norm_g.reshape(1, FOX_W))


SUPER = 256
GDN_HEADS_PER_STEP = 2
HALO = 8


def _split_bf16(x):
    hi = x.astype(BF16)
    return hi, (x - hi.astype(F32)).astype(BF16)


def _dot_split(a, b, dot=_dot):
    ah, al = a
    bh, bl = b
    return dot(ah, bh) + dot(ah, bl) + dot(al, bh)


def _gdn_body(xq_ref, xk_ref, xv_ref, z_ref, gc_ref, gr_ref, wq_ref, wk_ref, wv_ref, ng_ref,
              o_ref, s_ref, halo_ref, buf_ref, obuf_ref, *, tt):
    hps = GDN_HEADS_PER_STEP
    pair = pl.program_id(1)

    @pl.when(pl.program_id(2) == 0)
    def _():
        s_ref[...] = jnp.zeros_like(s_ref)
        halo_ref[...] = jnp.zeros_like(halo_ref)

    def conv_silu(x_ref, hh, idx, w_ref):
        lanes = slice(hh * LANES, (hh + 1) * LANES)
        slot = hh * 3 + idx
        x = x_ref[0, :, lanes]
        buf_ref[slot, 0:HALO, :] = halo_ref[slot]
        buf_ref[slot, HALO:HALO + tt, :] = x
        halo_ref[slot] = x[tt - HALO:tt, :]
        y = jnp.zeros_like(x)
        for j in range(CONV_K):
            off = HALO - (CONV_K - 1) + j
            y = y + w_ref[j:j + 1, lanes] * buf_ref[slot, off:off + tt, :]
        return _silu(y)

    gates = gc_ref[0]
    lane_t = lax.broadcasted_iota(jnp.int32, gates.shape, 1)
    ri = lax.broadcasted_iota(jnp.int32, (SUPER, SUPER), 0)
    ci = lax.broadcasted_iota(jnp.int32, (SUPER, SUPER), 1)
    same = (ri // GDN_CHUNK) == (ci // GDN_CHUNK)
    tril = same & (ci <= ri)
    strict = same & (ci < ri)
    eye = jnp.where(ri == ci, 1.0, 0.0).astype(F32)

    heads = []
    for hh in range(hps):
        head = hps * pair + hh
        q = conv_silu(xq_ref, hh, 0, wq_ref)
        k = conv_silu(xk_ref, hh, 1, wk_ref)
        v = conv_silu(xv_ref, hh, 2, wv_ref)
        q = q * lax.rsqrt(jnp.sum(q * q, axis=-1, keepdims=True) + EPS) * (GDN_DK ** -0.5)
        k = k * lax.rsqrt(jnp.sum(k * k, axis=-1, keepdims=True) + EPS)
        gcol = jnp.sum(jnp.where(lane_t == G_DEC + head, gates, 0.0), axis=1, keepdims=True)
        bcol = jnp.sum(jnp.where(lane_t == G_BETA + head, gates, 0.0), axis=1, keepdims=True)
        eg = jnp.exp(gcol)
        kb = k * bcol
        heads.append(dict(hh=hh, k=k, gcol=gcol, grow=gr_ref[0, hh], kb_split=_split_bf16(kb),
                          k_split=_split_bf16(k), k16=k.astype(BF16), q16=q.astype(BF16),
                          rhs16=jnp.concatenate([v * bcol, kb * eg], axis=1).astype(BF16),
                          qe16=(q * eg).astype(BF16)))

    blocks = []
    for hd in heads:
        for sc in range(tt // SUPER):
            rows = slice(sc * SUPER, (sc + 1) * SUPER)
            diff = hd["gcol"][rows] - hd["grow"][:, rows]
            decay = jnp.where(tril, jnp.exp(jnp.where(tril, diff, 0.0)), 0.0)
            gram = _dot_split(tuple(x[rows] for x in hd["kb_split"]),
                              tuple(x[rows] for x in hd["k_split"]), _dot_nt)
            a = jnp.where(strict, gram * decay, 0.0)
            blocks.append(dict(hd=hd, rows=rows, decay=decay, inv=eye - a, pw=_split_bf16(a)))

    for _ in range(5):
        for blk in blocks:
            blk["pw"] = _split_bf16(_dot_split(blk["pw"], blk["pw"]))
        for blk in blocks:
            blk["inv"] = blk["inv"] + _dot_split(_split_bf16(blk["inv"]), blk["pw"])

    for blk in blocks:
        hd, rows = blk["hd"], blk["rows"]
        inv_hi, inv_lo = _split_bf16(blk["inv"])
        blk["uw"] = _dot(inv_hi, hd["rhs16"][rows]) + _dot(inv_lo, hd["rhs16"][rows])
        blk["intra"] = jnp.where(tril, _dot_nt(hd["q16"][rows], hd["k16"][rows]) * blk["decay"],
                                 0.0).astype(BF16)

    for sc in range(tt // SUPER):
        for c in range(SUPER // GDN_CHUNK):
            lr = slice(c * GDN_CHUNK, (c + 1) * GDN_CHUNK)
            gr = slice(sc * SUPER + lr.start, sc * SUPER + lr.stop)
            for blk in blocks:
                if blk["rows"].start != sc * SUPER:
                    continue
                hd = blk["hd"]
                hh = hd["hh"]
                s = s_ref[hh]
                s16 = s.astype(BF16)
                v_new = blk["uw"][lr, :GDN_DV] - _dot(blk["uw"][lr, GDN_DV:].astype(BF16), s16)
                v16 = v_new.astype(BF16)
                obuf_ref[hh, gr, :] = _dot(hd["qe16"][gr], s16) + _dot(blk["intra"][lr, lr], v16)
                g_last = hd["gcol"][gr.stop - 1:gr.stop, :]
                k_dec = (hd["k"][gr] * jnp.exp(g_last - hd["gcol"][gr])).astype(BF16)
                s_ref[hh] = s * jnp.exp(g_last) + _dot_tn(k_dec, v16)

    for hh in range(hps):
        lanes = slice(hh * LANES, (hh + 1) * LANES)
        o = _rms(obuf_ref[hh], ng_ref[...]) * _silu(z_ref[0, :, lanes])
        o_ref[0, :, lanes] = o.astype(o_ref.dtype)


def _gdn(rest3, gates3, gates_t, conv_w, norm_g, tt=512):
    b, t, _ = rest3.shape
    hps = GDN_HEADS_PER_STEP
    npair = GDN_HEADS // hps
    wide = hps * LANES
    gr4 = gates_t.reshape(b, G_ROWS, 1, t)
    x_spec = lambda off: pl.BlockSpec((1, tt, wide), lambda i, p, j: (i, j, off + p))
    w_spec = lambda off: pl.BlockSpec((CONV_K, wide), lambda i, p, j: (0, off + p))
    return pl.pallas_call(
        functools.partial(_gdn_body, tt=tt),
        grid=(b, npair, t // tt),
        in_specs=[x_spec(0), x_spec(npair), x_spec(2 * npair), x_spec(3 * npair),
                  pl.BlockSpec((1, tt, LANES), lambda i, p, j: (i, j, 0)),
                  pl.BlockSpec((1, hps, 1, tt), lambda i, p, j: (i, G_DEC // hps + p, 0, j)),
                  w_spec(0), w_spec(npair), w_spec(2 * npair),
                  pl.BlockSpec((1, LANES), lambda i, p, j: (0, 0))],
        out_specs=pl.BlockSpec((1, tt, wide), lambda i, p, j: (i, j, p)),
        out_shape=jax.ShapeDtypeStruct((b, t, GDN_W), BF16),
        scratch_shapes=[pltpu.VMEM((hps, GDN_DK, GDN_DV), F32),
                        pltpu.VMEM((hps * 3, HALO, LANES), F32),
                        pltpu.VMEM((hps * 3, tt + HALO, LANES), F32),
                        pltpu.VMEM((hps, tt, LANES), F32)],
        compiler_params=_params(("parallel", "parallel", "arbitrary"), 48),
        name="gated_delta",
    )(rest3, rest3, rest3, rest3, gates3, gr4, conv_w, conv_w, conv_w, norm_g.reshape(1, GDN_DV))


def _outproj_body(x_ref, fo_ref, go_ref, w_ref, g_ref, xo_ref, *maybe_h_ref):
    y = _dot(fo_ref[...], w_ref[0:FOX_W, :]) + _dot(go_ref[...], w_ref[FOX_W:FOX_W + GDN_W, :])
    xn = x_ref[...] + y
    xo_ref[...] = xn
    if maybe_h_ref:
        maybe_h_ref[0][...] = _rms(xn, g_ref[...]).astype(BF16)


def _outproj(x, fo, go, w_out, ln_g, want_h, tm=512):
    n, d = x.shape
    row = lambda w: pl.BlockSpec((tm, w), lambda i: (i, 0))
    out_specs = [row(d)]
    out_shape = [jax.ShapeDtypeStruct((n, d), F32)]
    if want_h:
        out_specs.append(row(d))
        out_shape.append(jax.ShapeDtypeStruct((n, d), BF16))
    res = pl.pallas_call(
        _outproj_body,
        grid=(n // tm,),
        in_specs=[row(d), row(FOX_W), row(GDN_W), _resident((FOX_W + GDN_W, d)),
                  pl.BlockSpec((1, d), lambda i: (0, 0))],
        out_specs=out_specs,
        out_shape=out_shape,
        compiler_params=_params(("parallel",), 32),
        name="outproj",
    )(x, fo, go, w_out, ln_g.reshape(1, d))
    return res if want_h else (res[0], None)


def _ffn_body(x_ref, h_ref, w1_ref, w3_ref, w2_ref, g_ref, xo_ref, ho_ref, acc_ref, *, f_chunk):
    h = h_ref[...]
    d_ff = w1_ref.shape[1]
    for i, c in enumerate(range(0, d_ff, f_chunk)):
        a = _dot(h, w1_ref[:, c:c + f_chunk])
        b = _dot(h, w3_ref[:, c:c + f_chunk])
        y = _dot((_silu(a) * b).astype(BF16), w2_ref[c:c + f_chunk, :])
        if i == 0:
            acc_ref[...] = y
        else:
            acc_ref[...] += y
    xn = x_ref[...] + acc_ref[...]
    xo_ref[...] = xn
    ho_ref[...] = _rms(xn, g_ref[...]).astype(ho_ref.dtype)


def _ffn(x, h, w1, w3, w2, next_g, tm=512, f_chunk=256):
    n, d = x.shape
    d_ff = w1.shape[1]
    row = pl.BlockSpec((tm, d), lambda i: (i, 0))
    return pl.pallas_call(
        functools.partial(_ffn_body, f_chunk=f_chunk),
        grid=(n // tm,),
        in_specs=[row, row, _resident((d, d_ff)), _resident((d, d_ff)), _resident((d_ff, d)),
                  pl.BlockSpec((1, d), lambda i: (0, 0))],
        out_specs=[row, row],
        out_shape=[jax.ShapeDtypeStruct((n, d), F32), jax.ShapeDtypeStruct((n, d), BF16)],
        scratch_shapes=[pltpu.VMEM((tm, d), F32)],
        compiler_params=_params(("parallel",), 52),
        name="dense_swiglu",
    )(x, h, w1, w3, w2, next_g.reshape(1, d))


R_E0, R_E1, R_W0, R_W1, R_RANK0, R_RANK1 = 0, 1, 2, 3, 4, 5


def _router_body(x_ref, g_ref, w_ref, info_ref, cnt_ref, carry_ref, *, tm):
    @pl.when(pl.program_id(0) == 0)
    def _():
        carry_ref[...] = jnp.zeros_like(carry_ref)

    h = _rms(x_ref[...], g_ref[...])
    logits = _dot(h, w_ref[...], HI)
    lane = lax.broadcasted_iota(jnp.int32, logits.shape, 1)
    valid = lane < N_EXPERTS
    lm = jnp.where(valid, logits, NEG_BIG)
    ex = jnp.exp(lm - jnp.max(lm, axis=1, keepdims=True))
    probs = jnp.where(valid, ex / jnp.sum(ex, axis=1, keepdims=True), -1.0)
    p0 = jnp.max(probs, axis=1, keepdims=True)
    e0 = jnp.min(jnp.where(probs == p0, lane, LANES), axis=1, keepdims=True)
    rest = jnp.where(lane == e0, -1.0, probs)
    p1 = jnp.max(rest, axis=1, keepdims=True)
    e1 = jnp.min(jnp.where(rest == p1, lane, LANES), axis=1, keepdims=True)
    denom = p0 + p1

    picked = jnp.where((lane == e0) | (lane == e1), 1.0, 0.0).astype(F32)
    ri = lax.broadcasted_iota(jnp.int32, (tm, tm), 0)
    ci = lax.broadcasted_iota(jnp.int32, (tm, tm), 1)
    before = jnp.where(ci < ri, 1.0, 0.0).astype(BF16)
    earlier = _dot(before, picked.astype(BF16)) + carry_ref[...]
    rank0 = jnp.sum(jnp.where(lane == e0, earlier, 0.0), axis=1, keepdims=True)
    rank1 = jnp.sum(jnp.where(lane == e1, earlier, 0.0), axis=1, keepdims=True)
    carry_ref[...] += jnp.sum(picked, axis=0, keepdims=True)

    info = jnp.zeros(logits.shape, F32)
    for col, val in ((R_E0, e0.astype(F32)), (R_E1, e1.astype(F32)), (R_W0, p0 / denom),
                     (R_W1, p1 / denom), (R_RANK0, rank0), (R_RANK1, rank1)):
        info = jnp.where(lane == col, val, info)
    info_ref[...] = info
    cnt_ref[...] = jnp.broadcast_to(carry_ref[...], cnt_ref.shape)


def _router(x, ln_g, router_w_pad, tm=512):
    n, d = x.shape
    return pl.pallas_call(
        functools.partial(_router_body, tm=tm),
        grid=(n // tm,),
        in_specs=[pl.BlockSpec((tm, d), lambda i: (i, 0)),
                  pl.BlockSpec((1, d), lambda i: (0, 0)),
                  pl.BlockSpec((d, LANES), lambda i: (0, 0))],
        out_specs=[pl.BlockSpec((tm, LANES), lambda i: (i, 0)),
                   pl.BlockSpec((8, LANES), lambda i: (0, 0))],
        out_shape=[jax.ShapeDtypeStruct((n, LANES), F32),
                   jax.ShapeDtypeStruct((8, LANES), F32)],
        scratch_shapes=[pltpu.VMEM((1, LANES), F32)],
        compiler_params=_params(("arbitrary",), 32),
        name="moe_router",
    )(x, ln_g.reshape(1, d), router_w_pad)


def _row_copy(src_ref, src_row, dst_ref, dst_row, sem):
    return pltpu.make_async_copy(src_ref.at[pl.ds(src_row, 1)], dst_ref.at[pl.ds(dst_row, 1)], sem)


def _dispatch_body(pos_ref, x_ref, g_ref, xs_in_ref, xs_ref, buf_ref, sem_ref, *, tm):
    del xs_in_ref
    i = pl.program_id(0)
    slot = i % 2
    buf_ref[slot] = _rms(x_ref[...], g_ref[...])

    def issue(r, carry):
        base = 2 * (i * tm + r)
        for j in range(2):
            _row_copy(buf_ref.at[slot], r, xs_ref, pos_ref[base + j], sem_ref.at[slot]).start()
        return carry

    lax.fori_loop(0, tm, issue, 0, unroll=8)

    def wait_slot(s):
        for _ in range(2):
            pltpu.make_async_copy(buf_ref.at[s], xs_ref.at[pl.ds(0, tm)], sem_ref.at[s]).wait()

    @pl.when(i > 0)
    def _():
        wait_slot(1 - slot)

    @pl.when(i == pl.num_programs(0) - 1)
    def _():
        wait_slot(slot)


def _dispatch(x, ln_g, pos_flat, n_slots, tm=256):
    n, d = x.shape
    grid_spec = pltpu.PrefetchScalarGridSpec(
        num_scalar_prefetch=1,
        grid=(n // tm,),
        in_specs=[pl.BlockSpec((tm, d), lambda i, pos: (i, 0)),
                  pl.BlockSpec((1, d), lambda i, pos: (0, 0)),
                  pl.BlockSpec(memory_space=pl.ANY)],
        out_specs=pl.BlockSpec(memory_space=pl.ANY),
        scratch_shapes=[pltpu.VMEM((2, tm, d), F32), pltpu.SemaphoreType.DMA((2,))],
    )
    return pl.pallas_call(
        functools.partial(_dispatch_body, tm=tm),
        grid_spec=grid_spec,
        out_shape=jax.ShapeDtypeStruct((n_slots, d), F32),
        input_output_aliases={3: 0},
        compiler_params=_params(("arbitrary",), 32),
        name="moe_dispatch",
    )(pos_flat, x, ln_g.reshape(1, d), jnp.zeros((n_slots, d), F32))


def _experts_body(be_ref, nu_ref, xs_ref, w1_ref, w3_ref, w2_ref, o_ref, xb_ref, *, f_sub):
    del be_ref
    b = pl.program_id(0)
    f = pl.program_id(1)
    used = b < nu_ref[0]

    @pl.when(f == 0)
    def _():
        xb_ref[...] = xs_ref[...].astype(BF16)
        o_ref[...] = jnp.zeros_like(o_ref)

    @pl.when(used)
    def _():
        xb = xb_ref[...]
        for c in range(0, w1_ref.shape[2], f_sub):
            a = _dot(xb, w1_ref[0, :, c:c + f_sub])
            g = _dot(xb, w3_ref[0, :, c:c + f_sub])
            o_ref[...] += _dot((_silu(a) * g).astype(BF16), w2_ref[0, c:c + f_sub, :])


def _experts(xs, block_e, n_used, w1, w3, w2, f_chunk=1792, f_sub=256):
    p, d = xs.shape
    d_ff = w1.shape[2]
    nb = p // MOE_BLOCK
    grid_spec = pltpu.PrefetchScalarGridSpec(
        num_scalar_prefetch=2,
        grid=(nb, d_ff // f_chunk),
        in_specs=[pl.BlockSpec((MOE_BLOCK, d), lambda b, f, be, nu: (b, 0)),
                  pl.BlockSpec((1, d, f_chunk), lambda b, f, be, nu: (be[b], 0, f)),
                  pl.BlockSpec((1, d, f_chunk), lambda b, f, be, nu: (be[b], 0, f)),
                  pl.BlockSpec((1, f_chunk, d), lambda b, f, be, nu: (be[b], f, 0))],
        out_specs=pl.BlockSpec((MOE_BLOCK, d), lambda b, f, be, nu: (b, 0)),
        scratch_shapes=[pltpu.VMEM((MOE_BLOCK, d), BF16)],
    )
    return pl.pallas_call(
        functools.partial(_experts_body, f_sub=f_sub),
        grid_spec=grid_spec,
        out_shape=jax.ShapeDtypeStruct((p, d), F32),
        compiler_params=_params(("arbitrary", "arbitrary"), 48),
        name="moe_experts",
    )(block_e, n_used, xs, w1, w3, w2)


def _combine_body(pos_ref, x_ref, info_ref, g_ref, ys_ref, xo_ref, ho_ref, y_ref, sem_ref, *, tm):
    i = pl.program_id(0)

    def issue(r, carry):
        base = 2 * (i * tm + r)
        for j in range(2):
            _row_copy(ys_ref, pos_ref[base + j], y_ref.at[j], r, sem_ref.at[j]).start()
        return carry

    lax.fori_loop(0, tm, issue, 0, unroll=8)
    for j in range(2):
        pltpu.make_async_copy(ys_ref.at[pl.ds(0, tm)], y_ref.at[j], sem_ref.at[j]).wait()

    info = info_ref[...]
    w0 = info[:, R_W0:R_W0 + 1]
    w1 = info[:, R_W1:R_W1 + 1]
    xn = x_ref[...] + (w0 * y_ref[0] + w1 * y_ref[1])
    xo_ref[...] = xn
    ho_ref[...] = _rms(xn, g_ref[...]).astype(ho_ref.dtype)


def _combine(x, info, ys, pos_flat, next_g, h_dtype, tm=256):
    n, d = x.shape
    grid_spec = pltpu.PrefetchScalarGridSpec(
        num_scalar_prefetch=1,
        grid=(n // tm,),
        in_specs=[pl.BlockSpec((tm, d), lambda i, pos: (i, 0)),
                  pl.BlockSpec((tm, LANES), lambda i, pos: (i, 0)),
                  pl.BlockSpec((1, d), lambda i, pos: (0, 0)),
                  pl.BlockSpec(memory_space=pl.ANY)],
        out_specs=[pl.BlockSpec((tm, d), lambda i, pos: (i, 0)),
                   pl.BlockSpec((tm, d), lambda i, pos: (i, 0))],
        scratch_shapes=[pltpu.VMEM((2, tm, d), F32), pltpu.SemaphoreType.DMA((2,))],
    )
    return pl.pallas_call(
        functools.partial(_combine_body, tm=tm),
        grid_spec=grid_spec,
        out_shape=[jax.ShapeDtypeStruct((n, d), F32), jax.ShapeDtypeStruct((n, d), h_dtype)],
        compiler_params=_params(("arbitrary",), 32),
        name="moe_combine",
    )(pos_flat, x, info, next_g.reshape(1, d), ys)


def _moe(x, ln_g, router_w, w1, w3, w2, next_g, h_dtype):
    n, d = x.shape
    router_w_pad = jnp.pad(router_w, ((0, 0), (0, LANES - N_EXPERTS)))
    info, cnt = _router(x, ln_g, router_w_pad)

    counts = cnt[0, :N_EXPERTS].astype(jnp.int32)
    padded = ((counts + MOE_BLOCK - 1) // MOE_BLOCK) * MOE_BLOCK
    pend = jnp.cumsum(padded)
    pstart = pend - padded
    experts = info[:, R_E0:R_E1 + 1].astype(jnp.int32)
    ranks = info[:, R_RANK0:R_RANK1 + 1].astype(jnp.int32)
    pos_flat = (pstart[experts] + ranks).reshape(-1)
    n_blocks = -(-(2 * n) // MOE_BLOCK) + N_EXPERTS
    block_e = jnp.clip(jnp.searchsorted(pend, jnp.arange(n_blocks, dtype=jnp.int32) * MOE_BLOCK,
                                        side="right"), 0, N_EXPERTS - 1).astype(jnp.int32)
    n_used = (pend[-1:] // MOE_BLOCK).astype(jnp.int32)

    xs = _dispatch(x, ln_g, pos_flat, n_blocks * MOE_BLOCK)
    ys = _experts(xs, block_e, n_used, w1, w3, w2)
    return _combine(x, info, ys, pos_flat, next_g, h_dtype)


def _in_weights(w_in):
    o = 3 * FOX_W
    ff = w_in[:, o:o + FOX_HEADS]
    o += FOX_HEADS
    gqkv = w_in[:, o:o + REST_QKV]
    o += REST_QKV
    gab = w_in[:, o:o + 2 * GDN_HEADS]
    o += 2 * GDN_HEADS
    gz = w_in[:, o:o + GDN_W]
    fox = w_in[:, :3 * FOX_W]
    fox = jnp.concatenate([fox[:, :FOX_W] * (LOG2E * FOX_HEAD_DIM ** -0.5), fox[:, FOX_W:]], axis=1)
    pad = jnp.zeros((w_in.shape[0], LANES - G_ROWS), w_in.dtype)
    return jnp.concatenate([fox, gqkv, gz, ff, gab, pad], axis=1).astype(BF16)


def _gate_params(f_bias, dt_bias, a_log):
    row0 = jnp.zeros((LANES,), F32).at[G_FOX:G_FOX + FOX_HEADS].set(f_bias)
    row0 = row0.at[G_DEC:G_DEC + GDN_HEADS].set(dt_bias)
    row1 = jnp.zeros((LANES,), F32).at[G_DEC:G_DEC + GDN_HEADS].set(a_log)
    return jnp.zeros((8, LANES), F32).at[0].set(row0).at[1].set(row1)


def kernel(x, ln1_g, w_in, fox_f_bias, fox_norm_g, gdn_conv_w, gdn_a_log, gdn_dt_bias, gdn_norm_g,
           w_out, ln2_g, ffn_w1, ffn_w3, ffn_w2, router_w, exp_w1, exp_w3, exp_w2, final_g):
    b, t, d = x.shape
    n = b * t
    depth = w_in.shape[0]
    xr = x.reshape(n, d)
    h = _rmsnorm_rows(xr, ln1_g[0], BF16)
    for layer in range(depth):
        fox2, rest2 = _inproj(h, _in_weights(w_in[layer]))
        fox3 = fox2.reshape(b, t, 3 * FOX_W)
        rest3 = rest2.reshape(b, t, REST_W)
        gates3, gates_t, k_aug = _gates(rest3, _gate_params(fox_f_bias[layer], gdn_dt_bias[layer],
                                                            gdn_a_log[layer]))
        fo = _fox(fox3, gates_t, k_aug, fox_norm_g[layer])
        go = _gdn(rest3, gates3, gates_t, gdn_conv_w[layer], gdn_norm_g[layer])
        dense = layer % 2 == 0
        xr, h2 = _outproj(xr, fo.reshape(n, FOX_W), go.reshape(n, GDN_W),
                          w_out[layer].astype(BF16), ln2_g[layer], want_h=dense)
        last = layer == depth - 1
        next_g = final_g if last else ln1_g[layer + 1]
        j = layer // 2
        if dense:
            xr, h = _ffn(xr, h2, ffn_w1[j].astype(BF16), ffn_w3[j].astype(BF16),
                         ffn_w2[j].astype(BF16), next_g)
        else:
            xr, h = _moe(xr, ln2_g[layer], router_w[j], exp_w1[j].astype(BF16),
                         exp_w3[j].astype(BF16), exp_w2[j].astype(BF16), next_g,
                         F32 if last else BF16)
    return h.reshape(b, t, d)
```

```python
import functools

import jax
import jax.numpy as jnp
from jax import lax
from jax.experimental import pallas as pl
from jax.experimental.pallas import tpu as pltpu

F32 = jnp.float32
BF16 = jnp.bfloat16
HI = lax.Precision.HIGHEST

D_MODEL = 1024
FOX_HEADS = 8
FOX_HEAD_DIM = 64
FOX_W = FOX_HEADS * FOX_HEAD_DIM
GDN_HEADS = 4
GDN_DK = 128
GDN_DV = 128
GDN_W = GDN_HEADS * GDN_DK
CONV_K = 4
GDN_CHUNK = 64
N_EXPERTS = 8
MOE_BLOCK = 512
EPS = 1e-6

LANES = 128
NEG_BIG = -1e30
LOG2E = 1.4426950408889634
MIB = 1024 * 1024

REST_QKV = 3 * GDN_W
REST_Z = REST_QKV
REST_GATE = REST_QKV + GDN_W
REST_W = REST_GATE + LANES
GATE_BLK = REST_GATE // LANES
G_FOX = 0
G_DEC = FOX_HEADS
G_BETA = FOX_HEADS + GDN_HEADS
G_ROWS = FOX_HEADS + 2 * GDN_HEADS


def _dot(a, b, precision=None):
    return jnp.dot(a, b, preferred_element_type=F32, precision=precision)


def _dot_nt(a, b, precision=None):
    return lax.dot_general(a, b, (((1,), (1,)), ((), ())),
                           preferred_element_type=F32, precision=precision)


def _dot_tn(a, b, precision=None):
    return lax.dot_general(a, b, (((0,), (0,)), ((), ())),
                           preferred_element_type=F32, precision=precision)


def _params(semantics, vmem_mib):
    return pltpu.CompilerParams(dimension_semantics=semantics,
                                vmem_limit_bytes=vmem_mib * MIB)


def _rms(x, g):
    return x * lax.rsqrt(jnp.mean(x * x, axis=-1, keepdims=True) + EPS) * g


def _silu(x):
    return x * jax.nn.sigmoid(x)


def _resident(shape):
    nd = len(shape)
    return pl.BlockSpec(shape, lambda *_: (0,) * nd, pipeline_mode=pl.Buffered(1))


def _rms_body(x_ref, g_ref, o_ref):
    o_ref[...] = _rms(x_ref[...], g_ref[...]).astype(o_ref.dtype)


def _rmsnorm_rows(x, g, out_dtype, tm=1024):
    n, d = x.shape
    return pl.pallas_call(
        _rms_body,
        grid=(n // tm,),
        in_specs=[pl.BlockSpec((tm, d), lambda i: (i, 0)),
                  pl.BlockSpec((1, d), lambda i: (0, 0))],
        out_specs=pl.BlockSpec((tm, d), lambda i: (i, 0)),
        out_shape=jax.ShapeDtypeStruct((n, d), out_dtype),
        compiler_params=_params(("parallel",), 32),
        name="rmsnorm",
    )(x, g.reshape(1, d))


def _inproj_body(h_ref, w_ref, ofox_ref, orest_ref, *, col_chunk):
    h = h_ref[...]
    nf = ofox_ref.shape[1]
    nr = orest_ref.shape[1]
    for c in range(0, nf, col_chunk):
        ofox_ref[:, c:c + col_chunk] = _dot(h, w_ref[:, c:c + col_chunk]).astype(ofox_ref.dtype)
    for c in range(0, nr, col_chunk):
        e = min(c + col_chunk, nr)
        orest_ref[:, c:e] = _dot(h, w_ref[:, nf + c:nf + e])


def _inproj(h, w_all, tm=512):
    n, d = h.shape
    nf, nr = 3 * FOX_W, REST_W
    return pl.pallas_call(
        functools.partial(_inproj_body, col_chunk=512),
        grid=(n // tm,),
        in_specs=[pl.BlockSpec((tm, d), lambda i: (i, 0)),
                  _resident((d, nf + nr))],
        out_specs=[pl.BlockSpec((tm, nf), lambda i: (i, 0)),
                   pl.BlockSpec((tm, nr), lambda i: (i, 0))],
        out_shape=[jax.ShapeDtypeStruct((n, nf), BF16),
                   jax.ShapeDtypeStruct((n, nr), F32)],
        compiler_params=_params(("parallel",), 40),
        name="inproj",
    )(h, w_all)


def _gates_body(z_ref, p_ref, o_ref, ot_ref, ka_ref, carry_ref, *, tt):
    @pl.when(pl.program_id(1) == 0)
    def _():
        carry_ref[...] = jnp.zeros_like(carry_ref)

    z = z_ref[0] + p_ref[0:1, :]
    lane = lax.broadcasted_iota(jnp.int32, z.shape, 1)
    tail = jnp.log(1.0 + jnp.exp(-jnp.abs(z)))
    log_sig = jnp.minimum(z, 0.0) - tail
    softplus = jnp.maximum(z, 0.0) + tail
    decay = -jnp.exp(p_ref[1:2, :]) * softplus
    val = jnp.where(lane < G_DEC, log_sig, jnp.where(lane < G_BETA, decay, jax.nn.sigmoid(z)))

    ri = lax.broadcasted_iota(jnp.int32, (tt, tt), 0)
    ci = lax.broadcasted_iota(jnp.int32, (tt, tt), 1)
    tri = jnp.where(ci <= ri, 1.0, 0.0).astype(F32)
    blk = jnp.where((ci <= ri) & (ri // GDN_CHUNK == ci // GDN_CHUNK), 1.0, 0.0).astype(F32)
    full_cum = _dot(tri, val, HI) + carry_ref[...]
    chunk_cum = _dot(blk, val, HI)
    out = jnp.where(lane < G_DEC, full_cum, jnp.where(lane < G_BETA, chunk_cum, val))
    carry_ref[...] = full_cum[tt - 1:tt, :]
    o_ref[0] = out
    ot_ref[0] = out.T[:G_ROWS, :]

    neg_c = jnp.where(lane < G_DEC, -LOG2E * full_cum, 0.0)
    hi = neg_c.astype(BF16).astype(F32)
    mid = (neg_c - hi).astype(BF16).astype(F32)
    lo = (neg_c - hi - mid).astype(BF16).astype(F32)
    aug = hi + pltpu.roll(mid, FOX_HEADS, axis=1) + pltpu.roll(lo, 2 * FOX_HEADS, axis=1)
    ka_ref[0] = aug.astype(BF16)


def _gates(rest3, gate_params, tt=512):
    b, t, _ = rest3.shape
    return pl.pallas_call(
        functools.partial(_gates_body, tt=tt),
        grid=(b, t // tt),
        in_specs=[pl.BlockSpec((1, tt, LANES), lambda i, j: (i, j, GATE_BLK)),
                  pl.BlockSpec((8, LANES), lambda i, j: (0, 0))],
        out_specs=[pl.BlockSpec((1, tt, LANES), lambda i, j: (i, j, 0)),
                   pl.BlockSpec((1, G_ROWS, tt), lambda i, j: (i, 0, j)),
                   pl.BlockSpec((1, tt, LANES), lambda i, j: (i, j, 0))],
        out_shape=[jax.ShapeDtypeStruct((b, t, LANES), F32),
                   jax.ShapeDtypeStruct((b, G_ROWS, t), F32),
                   jax.ShapeDtypeStruct((b, t, LANES), BF16)],
        scratch_shapes=[pltpu.VMEM((1, LANES), F32)],
        compiler_params=_params(("parallel", "arbitrary"), 32),
        name="gates",
    )(rest3, gate_params)


def _fox_body(q_ref, k_ref, ka_ref, v_ref, c_ref, g_ref, o_ref, vt_ref, acc_ref, s_ref, cm_ref, m_ref,
              *, tq):
    hp = pl.program_id(1)
    qi = pl.program_id(2)
    half = FOX_HEAD_DIM
    t_total = k_ref.shape[1]
    lane_row = lax.broadcasted_iota(jnp.int32, (1, LANES), 1)
    sub = lax.broadcasted_iota(jnp.int32, (LANES, tq), 0)

    @pl.when(qi == 0)
    def _():
        for c in range(t_total // tq):
            vt = v_ref[0, c * tq:(c + 1) * tq, :].astype(F32).T
            cols = slice(c * tq, (c + 1) * tq)
            vt_ref[0, :, cols] = jnp.where(sub < half, vt, jnp.where(sub == half, 1.0, 0.0)).astype(BF16)
            vt_ref[1, :, cols] = jnp.where(sub >= half, vt, jnp.where(sub == 0, 1.0, 0.0)).astype(BF16)

    q = q_ref[0]
    zero = jnp.zeros_like(q)
    qs = pl.multiple_of(qi * tq, tq)
    q_aug, cq = [], []
    for j in range(2):
        head = 2 * hp + j
        pick = (lane_row == head) | (lane_row == FOX_HEADS + head) | (lane_row == 2 * FOX_HEADS + head)
        ones = jnp.broadcast_to(jnp.where(pick, 1.0, 0.0).astype(BF16), (tq, LANES))
        qj = jnp.where(lane_row < half, q, zero) if j == 0 else jnp.where(lane_row < half, zero, q)
        q_aug.append(jnp.concatenate([qj, ones], axis=1).astype(F32).T.astype(BF16))
        cq.append(c_ref[0, 0, j:j + 1, pl.ds(qs, tq)] * LOG2E)

    acc_ref[...] = jnp.zeros_like(acc_ref)
    m_ref[...] = jnp.full(m_ref.shape, NEG_BIG, F32)

    def scores(ki, j, diagonal):
        ks = pl.multiple_of(ki * tq, tq)
        k_aug = jnp.concatenate([k_ref[0, pl.ds(ks, tq), :], ka_ref[0, pl.ds(ks, tq), :]], axis=1)
        s = _dot(k_aug, q_aug[j])
        if diagonal:
            ri = lax.broadcasted_iota(jnp.int32, s.shape, 0)
            ci = lax.broadcasted_iota(jnp.int32, s.shape, 1)
            s = jnp.where(ri <= ci, s, NEG_BIG)
        return s

    def stash(j, s):
        s_ref[j] = s
        cm_ref[j] = jnp.max(s, axis=0, keepdims=True)

    def consume(ki, j):
        ks = pl.multiple_of(ki * tq, tq)
        m_old = m_ref[j]
        m_new = jnp.maximum(m_old, cq[j] + cm_ref[j])
        p = jnp.exp2(s_ref[j] + (cq[j] - m_new)).astype(BF16)
        acc_ref[j] = jnp.exp2(m_old - m_new) * acc_ref[j] + _dot(vt_ref[j, :, pl.ds(ks, tq)], p)
        m_ref[j] = m_new

    def advance(ki_next, diagonal):
        for j in range(2):
            s_next = scores(ki_next, j, diagonal)
            consume(ki_next - 1, j)
            stash(j, s_next)

    @pl.when(qi == 0)
    def _():
        for j in range(2):
            stash(j, scores(0, j, True))

    @pl.when(qi > 0)
    def _():
        for j in range(2):
            stash(j, scores(0, j, False))

    def steady(ki, carry):
        advance(ki + 1, False)
        return carry

    lax.fori_loop(0, qi - 1, steady, 0)

    @pl.when(qi > 0)
    def _():
        advance(qi, True)

    for j in range(2):
        consume(qi, j)

    a0 = acc_ref[0]
    a1 = acc_ref[1]
    ot = jnp.where(sub < half, a0 / a0[half:half + 1, :], a1 / a1[0:1, :])
    o = ot.T
    lo = lane_row < half
    sq = o * o
    ms0 = jnp.sum(jnp.where(lo, sq, 0.0), axis=1, keepdims=True) / half
    ms1 = jnp.sum(jnp.where(lo, 0.0, sq), axis=1, keepdims=True) / half
    inv = lax.rsqrt(jnp.where(lo, ms0, ms1) + EPS)
    o_ref[0] = (o * inv * g_ref[...]).astype(o_ref.dtype)


def _fox(fox3, gates_t, k_aug, norm_g, tq=512):
    b, t, _ = fox3.shape
    npair = FOX_HEADS // 2
    c4 = gates_t.reshape(b, G_ROWS // 2, 2, t)
    return pl.pallas_call(
        functools.partial(_fox_body, tq=tq),
        grid=(b, npair, t // tq),
        in_specs=[pl.BlockSpec((1, tq, LANES), lambda i, p, j: (i, j, p)),
                  pl.BlockSpec((1, t, LANES), lambda i, p, j: (i, 0, npair + p)),
                  pl.BlockSpec((1, t, LANES), lambda i, p, j: (i, 0, 0)),
                  pl.BlockSpec((1, t, LANES), lambda i, p, j: (i, 0, 2 * npair + p)),
                  pl.BlockSpec((1, 1, 2, t), lambda i, p, j: (i, p, 0, 0)),
                  pl.BlockSpec((1, LANES), lambda i, p, j: (0, p))],
        out_specs=pl.BlockSpec((1, tq, LANES), lambda i, p, j: (i, j, p)),
        out_shape=jax.ShapeDtypeStruct((b, t, FOX_W), BF16),
        scratch_shapes=[pltpu.VMEM((2, LANES, t), BF16),
                        pltpu.VMEM((2, LANES, tq), F32),
                        pltpu.VMEM((2, tq, tq), F32),
                        pltpu.VMEM((2, 1, tq), F32),
                        pltpu.VMEM((2, 1, tq), F32)],
        compiler_params=_params(("parallel", "parallel", "arbitrary"), 48),
        name="fox_attention",
    )(fox3, fox3, k_aug, fox3, c4, norm_g.reshape(1, FOX_W))


SUPER = 256
GDN_HEADS_PER_STEP = 4
HALO = 8


def _split_bf16(x):
    hi = x.astype(BF16)
    return hi, (x - hi.astype(F32)).astype(BF16)


def _dot_split(a, b, dot=_dot):
    ah, al = a
    bh, bl = b
    return dot(ah, bh) + dot(ah, bl) + dot(al, bh)


def _gdn_body(xq_ref, xk_ref, xv_ref, z_ref, gc_ref, gr_ref, wq_ref, wk_ref, wv_ref, ng_ref,
              o_ref, s_ref, halo_ref, buf_ref, obuf_ref, *, tt):
    hps = GDN_HEADS_PER_STEP
    pair = pl.program_id(1)

    @pl.when(pl.program_id(2) == 0)
    def _():
        s_ref[...] = jnp.zeros_like(s_ref)
        halo_ref[...] = jnp.zeros_like(halo_ref)

    def conv_silu(x_ref, hh, idx, w_ref):
        lanes = slice(hh * LANES, (hh + 1) * LANES)
        slot = hh * 3 + idx
        x = x_ref[0, :, lanes]
        buf_ref[slot, 0:HALO, :] = halo_ref[slot]
        buf_ref[slot, HALO:HALO + tt, :] = x
        halo_ref[slot] = x[tt - HALO:tt, :]
        y = jnp.zeros_like(x)
        for j in range(CONV_K):
            off = HALO - (CONV_K - 1) + j
            y = y + w_ref[j:j + 1, lanes] * buf_ref[slot, off:off + tt, :]
        return _silu(y)

    gates = gc_ref[0]
    lane_t = lax.broadcasted_iota(jnp.int32, gates.shape, 1)
    ri = lax.broadcasted_iota(jnp.int32, (SUPER, SUPER), 0)
    ci = lax.broadcasted_iota(jnp.int32, (SUPER, SUPER), 1)
    same = (ri // GDN_CHUNK) == (ci // GDN_CHUNK)
    tril = same & (ci <= ri)
    strict = same & (ci < ri)
    eye = jnp.where(ri == ci, 1.0, 0.0).astype(F32)

    heads = []
    for hh in range(hps):
        head = hps * pair + hh
        q = conv_silu(xq_ref, hh, 0, wq_ref)
        k = conv_silu(xk_ref, hh, 1, wk_ref)
        v = conv_silu(xv_ref, hh, 2, wv_ref)
        q = q * lax.rsqrt(jnp.sum(q * q, axis=-1, keepdims=True) + EPS) * (GDN_DK ** -0.5)
        k = k * lax.rsqrt(jnp.sum(k * k, axis=-1, keepdims=True) + EPS)
        gcol = jnp.sum(jnp.where(lane_t == G_DEC + head, gates, 0.0), axis=1, keepdims=True)
        bcol = jnp.sum(jnp.where(lane_t == G_BETA + head, gates, 0.0), axis=1, keepdims=True)
        eg = jnp.exp(gcol)
        kb = k * bcol
        heads.append(dict(hh=hh, k=k, gcol=gcol, grow=gr_ref[0, hh], kb_split=_split_bf16(kb),
                          k_split=_split_bf16(k), k16=k.astype(BF16), q16=q.astype(BF16),
                          rhs16=jnp.concatenate([v * bcol, kb * eg], axis=1).astype(BF16),
                          qe=q * eg))

    blocks = []
    for hd in heads:
        for sc in range(tt // SUPER):
            rows = slice(sc * SUPER, (sc + 1) * SUPER)
            diff = hd["gcol"][rows] - hd["grow"][:, rows]
            decay = jnp.where(tril, jnp.exp(jnp.where(tril, diff, 0.0)), 0.0)
            gram = _dot_split(tuple(x[rows] for x in hd["kb_split"]),
                              tuple(x[rows] for x in hd["k_split"]), _dot_nt)
            a = jnp.where(strict, gram * decay, 0.0)
            blocks.append(dict(hd=hd, rows=rows, decay=decay, inv=eye - a, pw=a.astype(BF16)))

    for _ in range(5):
        for blk in blocks:
            blk["pw"] = _dot(blk["pw"], blk["pw"]).astype(BF16)
        for blk in blocks:
            blk["inv"] = blk["inv"] + _dot(blk["inv"].astype(BF16), blk["pw"])

    chunks = SUPER // GDN_CHUNK
    for blk in blocks:
        hd, rows = blk["hd"], blk["rows"]
        inv_hi, inv_lo = _split_bf16(blk["inv"])
        uw16 = (_dot(inv_hi, hd["rhs16"][rows]) + _dot(inv_lo, hd["rhs16"][rows])).astype(BF16)
        intra = jnp.where(tril, _dot_nt(hd["q16"][rows], hd["k16"][rows]) * blk["decay"], 0.0)
        iuw = _dot(intra.astype(BF16), uw16)
        blk["o_fixed"] = iuw[:, :GDN_DV]
        blk["q_eff"] = (hd["qe"][rows] - iuw[:, GDN_DV:]).astype(BF16)
        blk["s_decay"], blk["s_add"], blk["s_mix"] = [], [], []
        for c in range(chunks):
            lr = slice(c * GDN_CHUNK, (c + 1) * GDN_CHUNK)
            gr = slice(rows.start + lr.start, rows.start + lr.stop)
            g_last = hd["gcol"][gr.stop - 1:gr.stop, :]
            k_dec = (hd["k"][gr] * jnp.exp(g_last - hd["gcol"][gr])).astype(BF16)
            kuw = _dot_tn(k_dec, uw16[lr])
            blk["s_decay"].append(jnp.exp(g_last))
            blk["s_add"].append(kuw[:, :GDN_DV])
            blk["s_mix"].append(kuw[:, GDN_DV:].astype(BF16))

    for sc in range(tt // SUPER):
        for c in range(chunks):
            lr = slice(c * GDN_CHUNK, (c + 1) * GDN_CHUNK)
            gr = slice(sc * SUPER + lr.start, sc * SUPER + lr.stop)
            for blk in blocks:
                if blk["rows"].start != sc * SUPER:
                    continue
                hh = blk["hd"]["hh"]
                s = s_ref[hh]
                s16 = s.astype(BF16)
                s_ref[hh] = s * blk["s_decay"][c] + blk["s_add"][c] - _dot(blk["s_mix"][c], s16)
                obuf_ref[hh, gr, :] = blk["o_fixed"][lr] + _dot(blk["q_eff"][lr], s16)

    for hh in range(hps):
        lanes = slice(hh * LANES, (hh + 1) * LANES)
        o = _rms(obuf_ref[hh], ng_ref[...]) * _silu(z_ref[0, :, lanes])
        o_ref[0, :, lanes] = o.astype(o_ref.dtype)


def _gdn(rest3, gates3, gates_t, conv_w, norm_g, tt=512):
    b, t, _ = rest3.shape
    hps = GDN_HEADS_PER_STEP
    npair = GDN_HEADS // hps
    wide = hps * LANES
    gr4 = gates_t.reshape(b, G_ROWS, 1, t)
    x_spec = lambda off: pl.BlockSpec((1, tt, wide), lambda i, p, j: (i, j, off + p))
    w_spec = lambda off: pl.BlockSpec((CONV_K, wide), lambda i, p, j: (0, off + p))
    return pl.pallas_call(
        functools.partial(_gdn_body, tt=tt),
        grid=(b, npair, t // tt),
        in_specs=[x_spec(0), x_spec(npair), x_spec(2 * npair), x_spec(3 * npair),
                  pl.BlockSpec((1, tt, LANES), lambda i, p, j: (i, j, 0)),
                  pl.BlockSpec((1, hps, 1, tt), lambda i, p, j: (i, G_DEC // hps + p, 0, j)),
                  w_spec(0), w_spec(npair), w_spec(2 * npair),
                  pl.BlockSpec((1, LANES), lambda i, p, j: (0, 0))],
        out_specs=pl.BlockSpec((1, tt, wide), lambda i, p, j: (i, j, p)),
        out_shape=jax.ShapeDtypeStruct((b, t, GDN_W), BF16),
        scratch_shapes=[pltpu.VMEM((hps, GDN_DK, GDN_DV), F32),
                        pltpu.VMEM((hps * 3, HALO, LANES), F32),
                        pltpu.VMEM((hps * 3, tt + HALO, LANES), F32),
                        pltpu.VMEM((hps, tt, LANES), F32)],
        compiler_params=_params(("parallel", "parallel", "arbitrary"), 48),
        name="gated_delta",
    )(rest3, rest3, rest3, rest3, gates3, gr4, conv_w, conv_w, conv_w, norm_g.reshape(1, GDN_DV))


def _outproj_body(x_ref, fo_ref, go_ref, w_ref, g_ref, xo_ref, *maybe_h_ref):
    y = _dot(fo_ref[...], w_ref[0:FOX_W, :]) + _dot(go_ref[...], w_ref[FOX_W:FOX_W + GDN_W, :])
    xn = x_ref[...] + y
    xo_ref[...] = xn
    if maybe_h_ref:
        maybe_h_ref[0][...] = _rms(xn, g_ref[...]).astype(BF16)


def _outproj(x, fo, go, w_out, ln_g, want_h, tm=512):
    n, d = x.shape
    row = lambda w: pl.BlockSpec((tm, w), lambda i: (i, 0))
    out_specs = [row(d)]
    out_shape = [jax.ShapeDtypeStruct((n, d), F32)]
    if want_h:
        out_specs.append(row(d))
        out_shape.append(jax.ShapeDtypeStruct((n, d), BF16))
    res = pl.pallas_call(
        _outproj_body,
        grid=(n // tm,),
        in_specs=[row(d), row(FOX_W), row(GDN_W), _resident((FOX_W + GDN_W, d)),
                  pl.BlockSpec((1, d), lambda i: (0, 0))],
        out_specs=out_specs,
        out_shape=out_shape,
        compiler_params=_params(("parallel",), 32),
        name="outproj",
    )(x, fo, go, w_out, ln_g.reshape(1, d))
    return res if want_h else (res[0], None)


def _ffn_body(x_ref, h_ref, w1_ref, w3_ref, w2_ref, g_ref, xo_ref, ho_ref, acc_ref, *, f_chunk):
    h = h_ref[...]
    d_ff = w1_ref.shape[1]
    for i, c in enumerate(range(0, d_ff, f_chunk)):
        a = _dot(h, w1_ref[:, c:c + f_chunk])
        b = _dot(h, w3_ref[:, c:c + f_chunk])
        y = _dot((_silu(a) * b).astype(BF16), w2_ref[c:c + f_chunk, :])
        if i == 0:
            acc_ref[...] = y
        else:
            acc_ref[...] += y
    xn = x_ref[...] + acc_ref[...]
    xo_ref[...] = xn
    ho_ref[...] = _rms(xn, g_ref[...]).astype(ho_ref.dtype)


def _ffn(x, h, w1, w3, w2, next_g, tm=512, f_chunk=256):
    n, d = x.shape
    d_ff = w1.shape[1]
    row = pl.BlockSpec((tm, d), lambda i: (i, 0))
    return pl.pallas_call(
        functools.partial(_ffn_body, f_chunk=f_chunk),
        grid=(n // tm,),
        in_specs=[row, row, _resident((d, d_ff)), _resident((d, d_ff)), _resident((d_ff, d)),
                  pl.BlockSpec((1, d), lambda i: (0, 0))],
        out_specs=[row, row],
        out_shape=[jax.ShapeDtypeStruct((n, d), F32), jax.ShapeDtypeStruct((n, d), BF16)],
        scratch_shapes=[pltpu.VMEM((tm, d), F32)],
        compiler_params=_params(("parallel",), 52),
        name="dense_swiglu",
    )(x, h, w1, w3, w2, next_g.reshape(1, d))


R_E0, R_E1, R_W0, R_W1, R_RANK0, R_RANK1 = 0, 1, 2, 3, 4, 5


def _router_body(x_ref, g_ref, w_ref, info_ref, cnt_ref, carry_ref, *, tm):
    @pl.when(pl.program_id(0) == 0)
    def _():
        carry_ref[...] = jnp.zeros_like(carry_ref)

    h = _rms(x_ref[...], g_ref[...])
    logits = _dot(h, w_ref[...], HI)
    lane = lax.broadcasted_iota(jnp.int32, logits.shape, 1)
    valid = lane < N_EXPERTS
    lm = jnp.where(valid, logits, NEG_BIG)
    ex = jnp.exp(lm - jnp.max(lm, axis=1, keepdims=True))
    probs = jnp.where(valid, ex / jnp.sum(ex, axis=1, keepdims=True), -1.0)
    p0 = jnp.max(probs, axis=1, keepdims=True)
    e0 = jnp.min(jnp.where(probs == p0, lane, LANES), axis=1, keepdims=True)
    rest = jnp.where(lane == e0, -1.0, probs)
    p1 = jnp.max(rest, axis=1, keepdims=True)
    e1 = jnp.min(jnp.where(rest == p1, lane, LANES), axis=1, keepdims=True)
    denom = p0 + p1

    picked = jnp.where((lane == e0) | (lane == e1), 1.0, 0.0).astype(F32)
    ri = lax.broadcasted_iota(jnp.int32, (tm, tm), 0)
    ci = lax.broadcasted_iota(jnp.int32, (tm, tm), 1)
    before = jnp.where(ci < ri, 1.0, 0.0).astype(BF16)
    earlier = _dot(before, picked.astype(BF16)) + carry_ref[...]
    rank0 = jnp.sum(jnp.where(lane == e0, earlier, 0.0), axis=1, keepdims=True)
    rank1 = jnp.sum(jnp.where(lane == e1, earlier, 0.0), axis=1, keepdims=True)
    carry_ref[...] += jnp.sum(picked, axis=0, keepdims=True)

    info = jnp.zeros(logits.shape, F32)
    for col, val in ((R_E0, e0.astype(F32)), (R_E1, e1.astype(F32)), (R_W0, p0 / denom),
                     (R_W1, p1 / denom), (R_RANK0, rank0), (R_RANK1, rank1)):
        info = jnp.where(lane == col, val, info)
    info_ref[...] = info
    cnt_ref[...] = jnp.broadcast_to(carry_ref[...], cnt_ref.shape)


def _router(x, ln_g, router_w_pad, tm=512):
    n, d = x.shape
    return pl.pallas_call(
        functools.partial(_router_body, tm=tm),
        grid=(n // tm,),
        in_specs=[pl.BlockSpec((tm, d), lambda i: (i, 0)),
                  pl.BlockSpec((1, d), lambda i: (0, 0)),
                  pl.BlockSpec((d, LANES), lambda i: (0, 0))],
        out_specs=[pl.BlockSpec((tm, LANES), lambda i: (i, 0)),
                   pl.BlockSpec((8, LANES), lambda i: (0, 0))],
        out_shape=[jax.ShapeDtypeStruct((n, LANES), F32),
                   jax.ShapeDtypeStruct((8, LANES), F32)],
        scratch_shapes=[pltpu.VMEM((1, LANES), F32)],
        compiler_params=_params(("arbitrary",), 32),
        name="moe_router",
    )(x, ln_g.reshape(1, d), router_w_pad)


def _row_copy(src_ref, src_row, dst_ref, dst_row, sem):
    return pltpu.make_async_copy(src_ref.at[pl.ds(src_row, 1)], dst_ref.at[pl.ds(dst_row, 1)], sem)


def _dispatch_body(pos_ref, x_ref, g_ref, xs_in_ref, xs_ref, buf_ref, sem_ref, *, tm):
    del xs_in_ref
    i = pl.program_id(0)
    slot = i % 2
    buf_ref[slot] = _rms(x_ref[...], g_ref[...])

    def issue(r, carry):
        base = 2 * (i * tm + r)
        for j in range(2):
            _row_copy(buf_ref.at[slot], r, xs_ref, pos_ref[base + j], sem_ref.at[slot]).start()
        return carry

    lax.fori_loop(0, tm, issue, 0, unroll=8)

    def wait_slot(s):
        for _ in range(2):
            pltpu.make_async_copy(buf_ref.at[s], xs_ref.at[pl.ds(0, tm)], sem_ref.at[s]).wait()

    @pl.when(i > 0)
    def _():
        wait_slot(1 - slot)

    @pl.when(i == pl.num_programs(0) - 1)
    def _():
        wait_slot(slot)


def _dispatch(x, ln_g, pos_flat, n_slots, tm=256):
    n, d = x.shape
    grid_spec = pltpu.PrefetchScalarGridSpec(
        num_scalar_prefetch=1,
        grid=(n // tm,),
        in_specs=[pl.BlockSpec((tm, d), lambda i, pos: (i, 0)),
                  pl.BlockSpec((1, d), lambda i, pos: (0, 0)),
                  pl.BlockSpec(memory_space=pl.ANY)],
        out_specs=pl.BlockSpec(memory_space=pl.ANY),
        scratch_shapes=[pltpu.VMEM((2, tm, d), F32), pltpu.SemaphoreType.DMA((2,))],
    )
    return pl.pallas_call(
        functools.partial(_dispatch_body, tm=tm),
        grid_spec=grid_spec,
        out_shape=jax.ShapeDtypeStruct((n_slots, d), F32),
        input_output_aliases={3: 0},
        compiler_params=_params(("arbitrary",), 32),
        name="moe_dispatch",
    )(pos_flat, x, ln_g.reshape(1, d), jnp.zeros((n_slots, d), F32))


def _experts_body(be_ref, nu_ref, xs_ref, w1_ref, w3_ref, w2_ref, o_ref, xb_ref, *, f_sub):
    del be_ref
    b = pl.program_id(0)
    f = pl.program_id(1)
    used = b < nu_ref[0]

    @pl.when(f == 0)
    def _():
        xb_ref[...] = xs_ref[...].astype(BF16)
        o_ref[...] = jnp.zeros_like(o_ref)

    @pl.when(used)
    def _():
        xb = xb_ref[...]
        for c in range(0, w1_ref.shape[1], f_sub):
            a = _dot(xb, w1_ref[:, c:c + f_sub])
            g = _dot(xb, w3_ref[:, c:c + f_sub])
            o_ref[...] += _dot((_silu(a) * g).astype(BF16), w2_ref[c:c + f_sub, :])


def _experts(xs, block_e, n_used, w1, w3, w2, moe_layer, f_chunk=1792, f_sub=256):
    p, d = xs.shape
    d_ff = w1.shape[3]
    nb = p // MOE_BLOCK
    grid_spec = pltpu.PrefetchScalarGridSpec(
        num_scalar_prefetch=2,
        grid=(nb, d_ff // f_chunk),
        in_specs=[pl.BlockSpec((MOE_BLOCK, d), lambda b, f, be, nu: (b, 0)),
                  pl.BlockSpec((None, None, d, f_chunk), lambda b, f, be, nu: (moe_layer, be[b], 0, f)),
                  pl.BlockSpec((None, None, d, f_chunk), lambda b, f, be, nu: (moe_layer, be[b], 0, f)),
                  pl.BlockSpec((None, None, f_chunk, d), lambda b, f, be, nu: (moe_layer, be[b], f, 0))],
        out_specs=pl.BlockSpec((MOE_BLOCK, d), lambda b, f, be, nu: (b, 0)),
        scratch_shapes=[pltpu.VMEM((MOE_BLOCK, d), BF16)],
    )
    return pl.pallas_call(
        functools.partial(_experts_body, f_sub=f_sub),
        grid_spec=grid_spec,
        out_shape=jax.ShapeDtypeStruct((p, d), F32),
        compiler_params=_params(("arbitrary", "arbitrary"), 48),
        name="moe_experts",
    )(block_e, n_used, xs, w1, w3, w2)


def _combine_body(pos_ref, x_ref, info_ref, g_ref, ys_ref, xo_ref, ho_ref, y_ref, sem_ref, *, tm):
    i = pl.program_id(0)

    def issue(r, carry):
        base = 2 * (i * tm + r)
        for j in range(2):
            _row_copy(ys_ref, pos_ref[base + j], y_ref.at[j], r, sem_ref.at[j]).start()
        return carry

    lax.fori_loop(0, tm, issue, 0, unroll=8)
    for j in range(2):
        pltpu.make_async_copy(ys_ref.at[pl.ds(0, tm)], y_ref.at[j], sem_ref.at[j]).wait()

    info = info_ref[...]
    w0 = info[:, R_W0:R_W0 + 1]
    w1 = info[:, R_W1:R_W1 + 1]
    xn = x_ref[...] + (w0 * y_ref[0] + w1 * y_ref[1])
    xo_ref[...] = xn
    ho_ref[...] = _rms(xn, g_ref[...]).astype(ho_ref.dtype)


def _combine(x, info, ys, pos_flat, next_g, h_dtype, tm=256):
    n, d = x.shape
    grid_spec = pltpu.PrefetchScalarGridSpec(
        num_scalar_prefetch=1,
        grid=(n // tm,),
        in_specs=[pl.BlockSpec((tm, d), lambda i, pos: (i, 0)),
                  pl.BlockSpec((tm, LANES), lambda i, pos: (i, 0)),
                  pl.BlockSpec((1, d), lambda i, pos: (0, 0)),
                  pl.BlockSpec(memory_space=pl.ANY)],
        out_specs=[pl.BlockSpec((tm, d), lambda i, pos: (i, 0)),
                   pl.BlockSpec((tm, d), lambda i, pos: (i, 0))],
        scratch_shapes=[pltpu.VMEM((2, tm, d), F32), pltpu.SemaphoreType.DMA((2,))],
    )
    return pl.pallas_call(
        functools.partial(_combine_body, tm=tm),
        grid_spec=grid_spec,
        out_shape=[jax.ShapeDtypeStruct((n, d), F32), jax.ShapeDtypeStruct((n, d), h_dtype)],
        compiler_params=_params(("arbitrary",), 32),
        name="moe_combine",
    )(pos_flat, x, info, next_g.reshape(1, d), ys)


def _moe(x, ln_g, router_w, w1, w3, w2, moe_layer, next_g, h_dtype):
    n, d = x.shape
    router_w_pad = jnp.pad(router_w, ((0, 0), (0, LANES - N_EXPERTS)))
    info, cnt = _router(x, ln_g, router_w_pad)

    counts = cnt[0, :N_EXPERTS].astype(jnp.int32)
    padded = ((counts + MOE_BLOCK - 1) // MOE_BLOCK) * MOE_BLOCK
    pend = jnp.cumsum(padded)
    pstart = pend - padded
    experts = info[:, R_E0:R_E1 + 1].astype(jnp.int32)
    ranks = info[:, R_RANK0:R_RANK1 + 1].astype(jnp.int32)
    pos_flat = (pstart[experts] + ranks).reshape(-1)
    n_blocks = -(-(2 * n) // MOE_BLOCK) + N_EXPERTS
    block_e = jnp.clip(jnp.searchsorted(pend, jnp.arange(n_blocks, dtype=jnp.int32) * MOE_BLOCK,
                                        side="right"), 0, N_EXPERTS - 1).astype(jnp.int32)
    n_used = (pend[-1:] // MOE_BLOCK).astype(jnp.int32)

    xs = _dispatch(x, ln_g, pos_flat, n_blocks * MOE_BLOCK)
    ys = _experts(xs, block_e, n_used, w1, w3, w2, moe_layer)
    return _combine(x, info, ys, pos_flat, next_g, h_dtype)


def _in_weights(w_in):
    o = 3 * FOX_W
    ff = w_in[:, o:o + FOX_HEADS]
    o += FOX_HEADS
    gqkv = w_in[:, o:o + REST_QKV]
    o += REST_QKV
    gab = w_in[:, o:o + 2 * GDN_HEADS]
    o += 2 * GDN_HEADS
    gz = w_in[:, o:o + GDN_W]
    fox = w_in[:, :3 * FOX_W]
    fox = jnp.concatenate([fox[:, :FOX_W] * (LOG2E * FOX_HEAD_DIM ** -0.5), fox[:, FOX_W:]], axis=1)
    pad = jnp.zeros((w_in.shape[0], LANES - G_ROWS), w_in.dtype)
    return jnp.concatenate([fox, gqkv, gz, ff, gab, pad], axis=1).astype(BF16)


def _gate_params(f_bias, dt_bias, a_log):
    row0 = jnp.zeros((LANES,), F32).at[G_FOX:G_FOX + FOX_HEADS].set(f_bias)
    row0 = row0.at[G_DEC:G_DEC + GDN_HEADS].set(dt_bias)
    row1 = jnp.zeros((LANES,), F32).at[G_DEC:G_DEC + GDN_HEADS].set(a_log)
    return jnp.zeros((8, LANES), F32).at[0].set(row0).at[1].set(row1)


def kernel(x, ln1_g, w_in, fox_f_bias, fox_norm_g, gdn_conv_w, gdn_a_log, gdn_dt_bias, gdn_norm_g,
           w_out, ln2_g, ffn_w1, ffn_w3, ffn_w2, router_w, exp_w1, exp_w3, exp_w2, final_g):
    b, t, d = x.shape
    n = b * t
    depth = w_in.shape[0]
    xr = x.reshape(n, d)
    ew1, ew3, ew2 = exp_w1.astype(BF16), exp_w3.astype(BF16), exp_w2.astype(BF16)
    h = _rmsnorm_rows(xr, ln1_g[0], BF16)
    for layer in range(depth):
        fox2, rest2 = _inproj(h, _in_weights(w_in[layer]))
        fox3 = fox2.reshape(b, t, 3 * FOX_W)
        rest3 = rest2.reshape(b, t, REST_W)
        gates3, gates_t, k_aug = _gates(rest3, _gate_params(fox_f_bias[layer], gdn_dt_bias[layer],
                                                            gdn_a_log[layer]))
        fo = _fox(fox3, gates_t, k_aug, fox_norm_g[layer])
        go = _gdn(rest3, gates3, gates_t, gdn_conv_w[layer], gdn_norm_g[layer])
        dense = layer % 2 == 0
        xr, h2 = _outproj(xr, fo.reshape(n, FOX_W), go.reshape(n, GDN_W),
                          w_out[layer].astype(BF16), ln2_g[layer], want_h=dense)
        last = layer == depth - 1
        next_g = final_g if last else ln1_g[layer + 1]
        j = layer // 2
        if dense:
            xr, h = _ffn(xr, h2, ffn_w1[j].astype(BF16), ffn_w3[j].astype(BF16),
                         ffn_w2[j].astype(BF16), next_g)
        else:
            xr, h = _moe(xr, ln2_g[layer], router_w[j], ew1, ew3, ew2, j, next_g,
                         F32 if last else BF16)
    return h.reshape(b, t, d)
```

```python
import functools

import jax
import jax.numpy as jnp
from jax import lax
from jax.experimental import pallas as pl
from jax.experimental.pallas import tpu as pltpu

F32 = jnp.float32
BF16 = jnp.bfloat16

D_MODEL = 1024
FOX_HEADS = 8
FOX_HEAD_DIM = 64
FOX_W = FOX_HEADS * FOX_HEAD_DIM
GDN_HEADS = 4
GDN_DK = 128
GDN_DV = 128
GDN_W = GDN_HEADS * GDN_DK
CONV_K = 4
GDN_CHUNK = 64
N_EXPERTS = 8
MOE_BLOCK = 512
EPS = 1e-6

LANES = 128
NEG_BIG = -1e30
LOG2E = 1.4426950408889634
MIB = 1024 * 1024

REST_QKV = 3 * GDN_W
REST_Z = REST_QKV
REST_GATE = REST_QKV + GDN_W
REST_W = REST_GATE + LANES
GATE_BLK = REST_GATE // LANES
G_FOX = 0
G_DEC = FOX_HEADS
G_BETA = FOX_HEADS + GDN_HEADS
G_ROWS = FOX_HEADS + 2 * GDN_HEADS


def _dot(a, b, precision=None):
    return jnp.dot(a, b, preferred_element_type=F32, precision=precision)


def _dot_nt(a, b, precision=None):
    return lax.dot_general(a, b, (((1,), (1,)), ((), ())),
                           preferred_element_type=F32, precision=precision)


def _dot_tn(a, b, precision=None):
    return lax.dot_general(a, b, (((0,), (0,)), ((), ())),
                           preferred_element_type=F32, precision=precision)


def _params(semantics, vmem_mib):
    return pltpu.CompilerParams(dimension_semantics=semantics,
                                vmem_limit_bytes=vmem_mib * MIB)


def _rms(x, g):
    return x * lax.rsqrt(jnp.mean(x * x, axis=-1, keepdims=True) + EPS) * g


def _silu(x):
    return x * jax.nn.sigmoid(x)


def _resident(shape):
    nd = len(shape)
    return pl.BlockSpec(shape, lambda *_: (0,) * nd, pipeline_mode=pl.Buffered(1))


def _rms_body(x_ref, g_ref, o_ref):
    o_ref[...] = _rms(x_ref[...], g_ref[...]).astype(o_ref.dtype)


def _rmsnorm_rows(x, g, out_dtype, tm=1024):
    n, d = x.shape
    return pl.pallas_call(
        _rms_body,
        grid=(n // tm,),
        in_specs=[pl.BlockSpec((tm, d), lambda i: (i, 0)),
                  pl.BlockSpec((1, d), lambda i: (0, 0))],
        out_specs=pl.BlockSpec((tm, d), lambda i: (i, 0)),
        out_shape=jax.ShapeDtypeStruct((n, d), out_dtype),
        compiler_params=_params(("parallel",), 32),
        name="rmsnorm",
    )(x, g.reshape(1, d))


def _inproj_body(h_ref, w_ref, ofox_ref, orest_ref, *, col_chunk):
    h = h_ref[...]
    nf = ofox_ref.shape[1]
    nr = orest_ref.shape[1]
    for c in range(0, nf, col_chunk):
        ofox_ref[:, c:c + col_chunk] = _dot(h, w_ref[:, c:c + col_chunk]).astype(ofox_ref.dtype)
    for c in range(0, nr, col_chunk):
        e = min(c + col_chunk, nr)
        orest_ref[:, c:e] = _dot(h, w_ref[:, nf + c:nf + e])


def _inproj(h, w_all, tm=512):
    n, d = h.shape
    nf, nr = 3 * FOX_W, REST_W
    return pl.pallas_call(
        functools.partial(_inproj_body, col_chunk=512),
        grid=(n // tm,),
        in_specs=[pl.BlockSpec((tm, d), lambda i: (i, 0)),
                  _resident((d, nf + nr))],
        out_specs=[pl.BlockSpec((tm, nf), lambda i: (i, 0)),
                   pl.BlockSpec((tm, nr), lambda i: (i, 0))],
        out_shape=[jax.ShapeDtypeStruct((n, nf), BF16),
                   jax.ShapeDtypeStruct((n, nr), F32)],
        compiler_params=_params(("parallel",), 40),
        name="inproj",
    )(h, w_all)


def _gates_body(z_ref, p_ref, o_ref, ot_ref, ka_ref, carry_ref, *, tt):
    @pl.when(pl.program_id(1) == 0)
    def _():
        carry_ref[...] = jnp.zeros_like(carry_ref)

    z = z_ref[0] + p_ref[0:1, :]
    lane = lax.broadcasted_iota(jnp.int32, z.shape, 1)
    tail = jnp.log(1.0 + jnp.exp(-jnp.abs(z)))
    log_sig = jnp.minimum(z, 0.0) - tail
    softplus = jnp.maximum(z, 0.0) + tail
    decay = -jnp.exp(p_ref[1:2, :]) * softplus
    val = jnp.where(lane < G_DEC, log_sig, jnp.where(lane < G_BETA, decay, jax.nn.sigmoid(z)))

    ri = lax.broadcasted_iota(jnp.int32, (tt, tt), 0)
    ci = lax.broadcasted_iota(jnp.int32, (tt, tt), 1)
    tri = jnp.where(ci <= ri, 1.0, 0.0)
    blk = jnp.where((ci <= ri) & (ri // GDN_CHUNK == ci // GDN_CHUNK), 1.0, 0.0)
    both = jnp.concatenate([tri, blk], axis=0).astype(BF16)
    v_hi = val.astype(BF16)
    v_mid = (val - v_hi.astype(F32)).astype(BF16)
    v_lo = (val - v_hi.astype(F32) - v_mid.astype(F32)).astype(BF16)
    hi_mid = _dot(both, jnp.concatenate([v_hi, v_mid], axis=1))
    cums = hi_mid[:, :LANES] + hi_mid[:, LANES:] + _dot(both, v_lo)
    full_cum = cums[:tt] + carry_ref[...]
    chunk_cum = cums[tt:]
    out = jnp.where(lane < G_DEC, full_cum, jnp.where(lane < G_BETA, chunk_cum, val))
    carry_ref[...] = full_cum[tt - 1:tt, :]
    o_ref[0] = out
    ot_ref[0] = out.T[:G_ROWS, :]

    neg_c = jnp.where(lane < G_DEC, -LOG2E * full_cum, 0.0)
    hi = neg_c.astype(BF16).astype(F32)
    mid = (neg_c - hi).astype(BF16).astype(F32)
    lo = (neg_c - hi - mid).astype(BF16).astype(F32)
    aug = hi + pltpu.roll(mid, FOX_HEADS, axis=1) + pltpu.roll(lo, 2 * FOX_HEADS, axis=1)
    ka_ref[0] = aug.astype(BF16)


def _gates(rest3, gate_params, tt=512):
    b, t, _ = rest3.shape
    return pl.pallas_call(
        functools.partial(_gates_body, tt=tt),
        grid=(b, t // tt),
        in_specs=[pl.BlockSpec((1, tt, LANES), lambda i, j: (i, j, GATE_BLK)),
                  pl.BlockSpec((8, LANES), lambda i, j: (0, 0))],
        out_specs=[pl.BlockSpec((1, tt, LANES), lambda i, j: (i, j, 0)),
                   pl.BlockSpec((1, G_ROWS, tt), lambda i, j: (i, 0, j)),
                   pl.BlockSpec((1, tt, LANES), lambda i, j: (i, j, 0))],
        out_shape=[jax.ShapeDtypeStruct((b, t, LANES), F32),
                   jax.ShapeDtypeStruct((b, G_ROWS, t), F32),
                   jax.ShapeDtypeStruct((b, t, LANES), BF16)],
        scratch_shapes=[pltpu.VMEM((1, LANES), F32)],
        compiler_params=_params(("parallel", "arbitrary"), 32),
        name="gates",
    )(rest3, gate_params)


def _fox_body(q_ref, k_ref, ka_ref, v_ref, c_ref, g_ref, o_ref, vt_ref, acc_ref, s_ref, cm_ref, m_ref,
              *, tq):
    hp = pl.program_id(1)
    qi = pl.program_id(2)
    half = FOX_HEAD_DIM
    t_total = k_ref.shape[1]
    lane_row = lax.broadcasted_iota(jnp.int32, (1, LANES), 1)
    sub = lax.broadcasted_iota(jnp.int32, (LANES, tq), 0)

    @pl.when(qi == 0)
    def _():
        for c in range(t_total // tq):
            vt = v_ref[0, c * tq:(c + 1) * tq, :].astype(F32).T
            cols = slice(c * tq, (c + 1) * tq)
            vt_ref[0, :, cols] = jnp.where(sub < half, vt, jnp.where(sub == half, 1.0, 0.0)).astype(BF16)
            vt_ref[1, :, cols] = jnp.where(sub >= half, vt, jnp.where(sub == 0, 1.0, 0.0)).astype(BF16)

    q = q_ref[0]
    zero = jnp.zeros_like(q)
    qs = pl.multiple_of(qi * tq, tq)
    q_aug, cq = [], []
    for j in range(2):
        head = 2 * hp + j
        pick = (lane_row == head) | (lane_row == FOX_HEADS + head) | (lane_row == 2 * FOX_HEADS + head)
        ones = jnp.broadcast_to(jnp.where(pick, 1.0, 0.0).astype(BF16), (tq, LANES))
        qj = jnp.where(lane_row < half, q, zero) if j == 0 else jnp.where(lane_row < half, zero, q)
        q_aug.append(jnp.concatenate([qj, ones], axis=1).astype(F32).T.astype(BF16))
        cq.append(c_ref[0, 0, j:j + 1, pl.ds(qs, tq)] * LOG2E)

    acc_ref[...] = jnp.zeros_like(acc_ref)
    m_ref[...] = jnp.full(m_ref.shape, NEG_BIG, F32)

    def scores(ki, j, diagonal):
        ks = pl.multiple_of(ki * tq, tq)
        k_aug = jnp.concatenate([k_ref[0, pl.ds(ks, tq), :], ka_ref[0, pl.ds(ks, tq), :]], axis=1)
        s = _dot(k_aug, q_aug[j])
        if diagonal:
            ri = lax.broadcasted_iota(jnp.int32, s.shape, 0)
            ci = lax.broadcasted_iota(jnp.int32, s.shape, 1)
            s = jnp.where(ri <= ci, s, NEG_BIG)
        return s

    def stash(j, s):
        s_ref[j] = s
        cm_ref[j] = jnp.max(s, axis=0, keepdims=True)

    def consume(ki, j):
        ks = pl.multiple_of(ki * tq, tq)
        m_old = m_ref[j]
        m_new = jnp.maximum(m_old, cq[j] + cm_ref[j])
        p = jnp.exp2(s_ref[j] + (cq[j] - m_new)).astype(BF16)
        acc_ref[j] = jnp.exp2(m_old - m_new) * acc_ref[j] + _dot(vt_ref[j, :, pl.ds(ks, tq)], p)
        m_ref[j] = m_new

    def advance(ki_next, diagonal):
        for j in range(2):
            s_next = scores(ki_next, j, diagonal)
            consume(ki_next - 1, j)
            stash(j, s_next)

    @pl.when(qi == 0)
    def _():
        for j in range(2):
            stash(j, scores(0, j, True))

    @pl.when(qi > 0)
    def _():
        for j in range(2):
            stash(j, scores(0, j, False))

    def steady(ki, carry):
        advance(ki + 1, False)
        return carry

    lax.fori_loop(0, qi - 1, steady, 0)

    @pl.when(qi > 0)
    def _():
        advance(qi, True)

    for j in range(2):
        consume(qi, j)

    a0 = acc_ref[0]
    a1 = acc_ref[1]
    ot = jnp.where(sub < half, a0 / a0[half:half + 1, :], a1 / a1[0:1, :])
    o = ot.T
    lo = lane_row < half
    sq = o * o
    ms0 = jnp.sum(jnp.where(lo, sq, 0.0), axis=1, keepdims=True) / half
    ms1 = jnp.sum(jnp.where(lo, 0.0, sq), axis=1, keepdims=True) / half
    inv = lax.rsqrt(jnp.where(lo, ms0, ms1) + EPS)
    o_ref[0] = (o * inv * g_ref[...]).astype(o_ref.dtype)


def _fox(fox3, gates_t, k_aug, norm_g, tq=512):
    b, t, _ = fox3.shape
    npair = FOX_HEADS // 2
    c4 = gates_t.reshape(b, G_ROWS // 2, 2, t)
    return pl.pallas_call(
        functools.partial(_fox_body, tq=tq),
        grid=(b, npair, t // tq),
        in_specs=[pl.BlockSpec((1, tq, LANES), lambda i, p, j: (i, j, p)),
                  pl.BlockSpec((1, t, LANES), lambda i, p, j: (i, 0, npair + p)),
                  pl.BlockSpec((1, t, LANES), lambda i, p, j: (i, 0, 0)),
                  pl.BlockSpec((1, t, LANES), lambda i, p, j: (i, 0, 2 * npair + p)),
                  pl.BlockSpec((1, 1, 2, t), lambda i, p, j: (i, p, 0, 0)),
                  pl.BlockSpec((1, LANES), lambda i, p, j: (0, p))],
        out_specs=pl.BlockSpec((1, tq, LANES), lambda i, p, j: (i, j, p)),
        out_shape=jax.ShapeDtypeStruct((b, t, FOX_W), BF16),
        scratch_shapes=[pltpu.VMEM((2, LANES, t), BF16),
                        pltpu.VMEM((2, LANES, tq), F32),
                        pltpu.VMEM((2, tq, tq), F32),
                        pltpu.VMEM((2, 1, tq), F32),
                        pltpu.VMEM((2, 1, tq), F32)],
        compiler_params=_params(("parallel", "parallel", "arbitrary"), 48),
        name="fox_attention",
    )(fox3, fox3, k_aug, fox3, c4, norm_g.reshape(1, FOX_W))


SUPER = 256
GDN_HEADS_PER_STEP = 4
HALO = 8


def _split_bf16(x):
    hi = x.astype(BF16)
    return hi, (x - hi.astype(F32)).astype(BF16)


def _dot_split(a, b, dot=_dot):
    ah, al = a
    bh, bl = b
    return dot(ah, bh) + dot(ah, bl) + dot(al, bh)


def _gdn_body(xq_ref, xk_ref, xv_ref, z_ref, gc_ref, gr_ref, wq_ref, wk_ref, wv_ref, ng_ref,
              o_ref, s_ref, halo_ref, buf_ref, obuf_ref, *, tt):
    hps = GDN_HEADS_PER_STEP
    pair = pl.program_id(1)

    @pl.when(pl.program_id(2) == 0)
    def _():
        s_ref[...] = jnp.zeros_like(s_ref)
        halo_ref[...] = jnp.zeros_like(halo_ref)

    def conv_silu(x_ref, hh, idx, w_ref):
        lanes = slice(hh * LANES, (hh + 1) * LANES)
        slot = hh * 3 + idx
        x = x_ref[0, :, lanes]
        buf_ref[slot, 0:HALO, :] = halo_ref[slot]
        buf_ref[slot, HALO:HALO + tt, :] = x
        halo_ref[slot] = x[tt - HALO:tt, :]
        y = jnp.zeros_like(x)
        for j in range(CONV_K):
            off = HALO - (CONV_K - 1) + j
            y = y + w_ref[j:j + 1, lanes] * buf_ref[slot, off:off + tt, :]
        return _silu(y)

    gates = gc_ref[0]
    lane_t = lax.broadcasted_iota(jnp.int32, gates.shape, 1)
    ri = lax.broadcasted_iota(jnp.int32, (SUPER, SUPER), 0)
    ci = lax.broadcasted_iota(jnp.int32, (SUPER, SUPER), 1)
    same = (ri // GDN_CHUNK) == (ci // GDN_CHUNK)
    tril = same & (ci <= ri)
    strict = same & (ci < ri)
    eye = jnp.where(ri == ci, 1.0, 0.0).astype(F32)

    heads = []
    for hh in range(hps):
        head = hps * pair + hh
        q = conv_silu(xq_ref, hh, 0, wq_ref)
        k = conv_silu(xk_ref, hh, 1, wk_ref)
        v = conv_silu(xv_ref, hh, 2, wv_ref)
        q = q * lax.rsqrt(jnp.sum(q * q, axis=-1, keepdims=True) + EPS) * (GDN_DK ** -0.5)
        k = k * lax.rsqrt(jnp.sum(k * k, axis=-1, keepdims=True) + EPS)
        gcol = jnp.sum(jnp.where(lane_t == G_DEC + head, gates, 0.0), axis=1, keepdims=True)
        bcol = jnp.sum(jnp.where(lane_t == G_BETA + head, gates, 0.0), axis=1, keepdims=True)
        eg = jnp.exp(gcol)
        kb = k * bcol
        heads.append(dict(hh=hh, k=k, gcol=gcol, grow=gr_ref[0, hh], kb_split=_split_bf16(kb),
                          k_split=_split_bf16(k), k16=k.astype(BF16), q16=q.astype(BF16),
                          rhs16=jnp.concatenate([v * bcol, kb * eg], axis=1).astype(BF16),
                          qe=q * eg))

    blocks = []
    for hd in heads:
        for sc in range(tt // SUPER):
            rows = slice(sc * SUPER, (sc + 1) * SUPER)
            diff = hd["gcol"][rows] - hd["grow"][:, rows]
            decay = jnp.where(tril, jnp.exp(jnp.where(tril, diff, 0.0)), 0.0)
            gram = _dot_split(tuple(x[rows] for x in hd["kb_split"]),
                              tuple(x[rows] for x in hd["k_split"]), _dot_nt)
            a = jnp.where(strict, gram * decay, 0.0)
            blocks.append(dict(hd=hd, rows=rows, decay=decay, inv=eye - a, pw=a.astype(BF16)))

    for _ in range(5):
        for blk in blocks:
            blk["pw"] = _dot(blk["pw"], blk["pw"]).astype(BF16)
        for blk in blocks:
            blk["inv"] = blk["inv"] + _dot(blk["inv"].astype(BF16), blk["pw"])

    chunks = SUPER // GDN_CHUNK
    for blk in blocks:
        hd, rows = blk["hd"], blk["rows"]
        inv_hi, inv_lo = _split_bf16(blk["inv"])
        uw16 = (_dot(inv_hi, hd["rhs16"][rows]) + _dot(inv_lo, hd["rhs16"][rows])).astype(BF16)
        intra = jnp.where(tril, _dot_nt(hd["q16"][rows], hd["k16"][rows]) * blk["decay"], 0.0)
        iuw = _dot(intra.astype(BF16), uw16)
        blk["o_fixed"] = iuw[:, :GDN_DV]
        blk["q_eff"] = (hd["qe"][rows] - iuw[:, GDN_DV:]).astype(BF16)
        blk["s_decay"], blk["s_add"], blk["s_mix"] = [], [], []
        for c in range(chunks):
            lr = slice(c * GDN_CHUNK, (c + 1) * GDN_CHUNK)
            gr = slice(rows.start + lr.start, rows.start + lr.stop)
            g_last = hd["gcol"][gr.stop - 1:gr.stop, :]
            k_dec = (hd["k"][gr] * jnp.exp(g_last - hd["gcol"][gr])).astype(BF16)
            kuw = _dot_tn(k_dec, uw16[lr])
            blk["s_decay"].append(jnp.exp(g_last))
            blk["s_add"].append(kuw[:, :GDN_DV])
            blk["s_mix"].append(kuw[:, GDN_DV:].astype(BF16))

    for sc in range(tt // SUPER):
        for c in range(chunks):
            lr = slice(c * GDN_CHUNK, (c + 1) * GDN_CHUNK)
            gr = slice(sc * SUPER + lr.start, sc * SUPER + lr.stop)
            for blk in blocks:
                if blk["rows"].start != sc * SUPER:
                    continue
                hh = blk["hd"]["hh"]
                s = s_ref[hh]
                s16 = s.astype(BF16)
                s_ref[hh] = s * blk["s_decay"][c] + blk["s_add"][c] - _dot(blk["s_mix"][c], s16)
                obuf_ref[hh, gr, :] = blk["o_fixed"][lr] + _dot(blk["q_eff"][lr], s16)

    for hh in range(hps):
        lanes = slice(hh * LANES, (hh + 1) * LANES)
        o = _rms(obuf_ref[hh], ng_ref[...]) * _silu(z_ref[0, :, lanes])
        o_ref[0, :, lanes] = o.astype(o_ref.dtype)


def _gdn(rest3, gates3, gates_t, conv_w, norm_g, tt=512):
    b, t, _ = rest3.shape
    hps = GDN_HEADS_PER_STEP
    npair = GDN_HEADS // hps
    wide = hps * LANES
    gr4 = gates_t.reshape(b, G_ROWS, 1, t)
    x_spec = lambda off: pl.BlockSpec((1, tt, wide), lambda i, p, j: (i, j, off + p))
    w_spec = lambda off: pl.BlockSpec((CONV_K, wide), lambda i, p, j: (0, off + p))
    return pl.pallas_call(
        functools.partial(_gdn_body, tt=tt),
        grid=(b, npair, t // tt),
        in_specs=[x_spec(0), x_spec(npair), x_spec(2 * npair), x_spec(3 * npair),
                  pl.BlockSpec((1, tt, LANES), lambda i, p, j: (i, j, 0)),
                  pl.BlockSpec((1, hps, 1, tt), lambda i, p, j: (i, G_DEC // hps + p, 0, j)),
                  w_spec(0), w_spec(npair), w_spec(2 * npair),
                  pl.BlockSpec((1, LANES), lambda i, p, j: (0, 0))],
        out_specs=pl.BlockSpec((1, tt, wide), lambda i, p, j: (i, j, p)),
        out_shape=jax.ShapeDtypeStruct((b, t, GDN_W), BF16),
        scratch_shapes=[pltpu.VMEM((hps, GDN_DK, GDN_DV), F32),
                        pltpu.VMEM((hps * 3, HALO, LANES), F32),
                        pltpu.VMEM((hps * 3, tt + HALO, LANES), F32),
                        pltpu.VMEM((hps, tt, LANES), F32)],
        compiler_params=_params(("parallel", "parallel", "arbitrary"), 48),
        name="gated_delta",
    )(rest3, rest3, rest3, rest3, gates3, gr4, conv_w, conv_w, conv_w, norm_g.reshape(1, GDN_DV))


def _mixer_residual(x_ref, fo_ref, go_ref, wo_ref):
    y = _dot(fo_ref[...], wo_ref[0:FOX_W, :]) + _dot(go_ref[...], wo_ref[FOX_W:FOX_W + GDN_W, :])
    return x_ref[...] + y


def _ffn_body(x_ref, fo_ref, go_ref, wo_ref, g2_ref, w1_ref, w3_ref, w2_ref, g_ref,
              xo_ref, ho_ref, acc_ref, *, f_chunk):
    xm = _mixer_residual(x_ref, fo_ref, go_ref, wo_ref)
    h = _rms(xm, g2_ref[...]).astype(BF16)
    d_ff = w1_ref.shape[1]
    for i, c in enumerate(range(0, d_ff, f_chunk)):
        a = _dot(h, w1_ref[:, c:c + f_chunk])
        b = _dot(h, w3_ref[:, c:c + f_chunk])
        y = _dot((_silu(a) * b).astype(BF16), w2_ref[c:c + f_chunk, :])
        if i == 0:
            acc_ref[...] = y
        else:
            acc_ref[...] += y
    xn = xm + acc_ref[...]
    xo_ref[...] = xn
    ho_ref[...] = _rms(xn, g_ref[...]).astype(ho_ref.dtype)


def _ffn(x, fo, go, w_out, ln2_g, w1, w3, w2, next_g, h_dtype, tm=512, f_chunk=256):
    n, d = x.shape
    d_ff = w1.shape[1]
    row = lambda w: pl.BlockSpec((tm, w), lambda i: (i, 0))
    vec = pl.BlockSpec((1, d), lambda i: (0, 0))
    return pl.pallas_call(
        functools.partial(_ffn_body, f_chunk=f_chunk),
        grid=(n // tm,),
        in_specs=[row(d), row(FOX_W), row(GDN_W), _resident((FOX_W + GDN_W, d)), vec,
                  _resident((d, d_ff)), _resident((d, d_ff)), _resident((d_ff, d)), vec],
        out_specs=[row(d), row(d)],
        out_shape=[jax.ShapeDtypeStruct((n, d), F32), jax.ShapeDtypeStruct((n, d), h_dtype)],
        scratch_shapes=[pltpu.VMEM((tm, d), F32)],
        compiler_params=_params(("parallel",), 54),
        name="outproj_dense_swiglu",
    )(x, fo, go, w_out, ln2_g.reshape(1, d), w1, w3, w2, next_g.reshape(1, d))


R_E0, R_E1, R_W0, R_W1, R_RANK0, R_RANK1 = 0, 1, 2, 3, 4, 5


def _router_body(x_ref, fo_ref, go_ref, wo_ref, g_ref, w_ref, xo_ref, info_ref, cnt_ref, carry_ref,
                 *, tm):
    @pl.when(pl.program_id(0) == 0)
    def _():
        carry_ref[...] = jnp.zeros_like(carry_ref)

    xm = _mixer_residual(x_ref, fo_ref, go_ref, wo_ref)
    xo_ref[...] = xm
    h = _rms(xm, g_ref[...])
    h_hi, h_lo = _split_bf16(h)
    w_hi, w_lo = _split_bf16(w_ref[...])
    two = _dot(h_hi, jnp.concatenate([w_hi, w_lo], axis=1))
    logits = two[:, :LANES] + two[:, LANES:] + _dot(h_lo, w_hi)
    lane = lax.broadcasted_iota(jnp.int32, logits.shape, 1)
    valid = lane < N_EXPERTS
    lm = jnp.where(valid, logits, NEG_BIG)
    ex = jnp.exp(lm - jnp.max(lm, axis=1, keepdims=True))
    probs = jnp.where(valid, ex / jnp.sum(ex, axis=1, keepdims=True), -1.0)
    p0 = jnp.max(probs, axis=1, keepdims=True)
    e0 = jnp.min(jnp.where(probs == p0, lane, LANES), axis=1, keepdims=True)
    rest = jnp.where(lane == e0, -1.0, probs)
    p1 = jnp.max(rest, axis=1, keepdims=True)
    e1 = jnp.min(jnp.where(rest == p1, lane, LANES), axis=1, keepdims=True)
    denom = p0 + p1

    picked = jnp.where((lane == e0) | (lane == e1), 1.0, 0.0).astype(F32)
    ri = lax.broadcasted_iota(jnp.int32, (tm, tm), 0)
    ci = lax.broadcasted_iota(jnp.int32, (tm, tm), 1)
    before = jnp.where(ci < ri, 1.0, 0.0).astype(BF16)
    earlier = _dot(before, picked.astype(BF16)) + carry_ref[...]
    rank0 = jnp.sum(jnp.where(lane == e0, earlier, 0.0), axis=1, keepdims=True)
    rank1 = jnp.sum(jnp.where(lane == e1, earlier, 0.0), axis=1, keepdims=True)
    carry_ref[...] += jnp.sum(picked, axis=0, keepdims=True)

    info = jnp.zeros(logits.shape, F32)
    for col, val in ((R_E0, e0.astype(F32)), (R_E1, e1.astype(F32)), (R_W0, p0 / denom),
                     (R_W1, p1 / denom), (R_RANK0, rank0), (R_RANK1, rank1)):
        info = jnp.where(lane == col, val, info)
    info_ref[...] = info
    cnt_ref[...] = jnp.broadcast_to(carry_ref[...], cnt_ref.shape)


def _router(x, fo, go, w_out, ln_g, router_w_pad, tm=512):
    n, d = x.shape
    row = lambda w: pl.BlockSpec((tm, w), lambda i: (i, 0))
    return pl.pallas_call(
        functools.partial(_router_body, tm=tm),
        grid=(n // tm,),
        in_specs=[row(d), row(FOX_W), row(GDN_W), _resident((FOX_W + GDN_W, d)),
                  pl.BlockSpec((1, d), lambda i: (0, 0)),
                  pl.BlockSpec((d, LANES), lambda i: (0, 0))],
        out_specs=[row(d), row(LANES),
                   pl.BlockSpec((8, LANES), lambda i: (0, 0))],
        out_shape=[jax.ShapeDtypeStruct((n, d), F32),
                   jax.ShapeDtypeStruct((n, LANES), F32),
                   jax.ShapeDtypeStruct((8, LANES), F32)],
        scratch_shapes=[pltpu.VMEM((1, LANES), F32)],
        compiler_params=_params(("arbitrary",), 32),
        name="outproj_moe_router",
    )(x, fo, go, w_out, ln_g.reshape(1, d), router_w_pad)


def _row_copy(src_ref, src_row, dst_ref, dst_row, sem):
    return pltpu.make_async_copy(src_ref.at[pl.ds(src_row, 1)], dst_ref.at[pl.ds(dst_row, 1)], sem)


def _dispatch_body(pos_ref, x_ref, g_ref, xs_in_ref, xs_ref, buf_ref, sem_ref, *, tm):
    del xs_in_ref
    i = pl.program_id(0)
    slot = i % 2
    buf_ref[slot] = _rms(x_ref[...], g_ref[...])

    def issue(r, carry):
        base = 2 * (i * tm + r)
        for j in range(2):
            _row_copy(buf_ref.at[slot], r, xs_ref, pos_ref[base + j], sem_ref.at[slot]).start()
        return carry

    lax.fori_loop(0, tm, issue, 0, unroll=8)

    def wait_slot(s):
        for _ in range(2):
            pltpu.make_async_copy(buf_ref.at[s], xs_ref.at[pl.ds(0, tm)], sem_ref.at[s]).wait()

    @pl.when(i > 0)
    def _():
        wait_slot(1 - slot)

    @pl.when(i == pl.num_programs(0) - 1)
    def _():
        wait_slot(slot)


def _dispatch(x, ln_g, pos_flat, n_slots, tm=256):
    n, d = x.shape
    grid_spec = pltpu.PrefetchScalarGridSpec(
        num_scalar_prefetch=1,
        grid=(n // tm,),
        in_specs=[pl.BlockSpec((tm, d), lambda i, pos: (i, 0)),
                  pl.BlockSpec((1, d), lambda i, pos: (0, 0)),
                  pl.BlockSpec(memory_space=pl.ANY)],
        out_specs=pl.BlockSpec(memory_space=pl.ANY),
        scratch_shapes=[pltpu.VMEM((2, tm, d), F32), pltpu.SemaphoreType.DMA((2,))],
    )
    return pl.pallas_call(
        functools.partial(_dispatch_body, tm=tm),
        grid_spec=grid_spec,
        out_shape=jax.ShapeDtypeStruct((n_slots, d), F32),
        input_output_aliases={3: 0},
        compiler_params=_params(("arbitrary",), 32),
        name="moe_dispatch",
    )(pos_flat, x, ln_g.reshape(1, d), jnp.zeros((n_slots, d), F32))


def _experts_body(be_ref, nu_ref, xs_ref, w1_ref, w3_ref, w2_ref, o_ref, xb_ref, *, f_sub):
    del be_ref
    b = pl.program_id(0)
    f = pl.program_id(1)
    used = b < nu_ref[0]

    @pl.when(f == 0)
    def _():
        xb_ref[...] = xs_ref[...].astype(BF16)
        o_ref[...] = jnp.zeros_like(o_ref)

    @pl.when(used)
    def _():
        xb = xb_ref[...]
        for c in range(0, w1_ref.shape[1], f_sub):
            a = _dot(xb, w1_ref[:, c:c + f_sub])
            g = _dot(xb, w3_ref[:, c:c + f_sub])
            o_ref[...] += _dot((_silu(a) * g).astype(BF16), w2_ref[c:c + f_sub, :])


def _experts(xs, block_e, n_used, w1, w3, w2, moe_layer, f_chunk=1792, f_sub=256):
    p, d = xs.shape
    d_ff = w1.shape[3]
    nb = p // MOE_BLOCK
    grid_spec = pltpu.PrefetchScalarGridSpec(
        num_scalar_prefetch=2,
        grid=(nb, d_ff // f_chunk),
        in_specs=[pl.BlockSpec((MOE_BLOCK, d), lambda b, f, be, nu: (b, 0)),
                  pl.BlockSpec((None, None, d, f_chunk), lambda b, f, be, nu: (moe_layer, be[b], 0, f)),
                  pl.BlockSpec((None, None, d, f_chunk), lambda b, f, be, nu: (moe_layer, be[b], 0, f)),
                  pl.BlockSpec((None, None, f_chunk, d), lambda b, f, be, nu: (moe_layer, be[b], f, 0))],
        out_specs=pl.BlockSpec((MOE_BLOCK, d), lambda b, f, be, nu: (b, 0)),
        scratch_shapes=[pltpu.VMEM((MOE_BLOCK, d), BF16)],
    )
    return pl.pallas_call(
        functools.partial(_experts_body, f_sub=f_sub),
        grid_spec=grid_spec,
        out_shape=jax.ShapeDtypeStruct((p, d), F32),
        compiler_params=_params(("arbitrary", "arbitrary"), 48),
        name="moe_experts",
    )(block_e, n_used, xs, w1, w3, w2)


def _combine_body(pos_ref, x_ref, info_ref, g_ref, ys_ref, xo_ref, ho_ref, y_ref, sem_ref, *, tm):
    i = pl.program_id(0)
    slot = i % 2

    def gather(step, into):
        def issue(r, carry):
            base = 2 * (step * tm + r)
            for j in range(2):
                _row_copy(ys_ref, pos_ref[base + j], y_ref.at[into, j], r, sem_ref.at[into, j]).start()
            return carry

        lax.fori_loop(0, tm, issue, 0, unroll=8)

    @pl.when(i == 0)
    def _():
        gather(i, slot)

    @pl.when(i + 1 < pl.num_programs(0))
    def _():
        gather(i + 1, 1 - slot)

    for j in range(2):
        pltpu.make_async_copy(ys_ref.at[pl.ds(0, tm)], y_ref.at[slot, j], sem_ref.at[slot, j]).wait()

    info = info_ref[...]
    w0 = info[:, R_W0:R_W0 + 1]
    w1 = info[:, R_W1:R_W1 + 1]
    xn = x_ref[...] + (w0 * y_ref[slot, 0] + w1 * y_ref[slot, 1])
    xo_ref[...] = xn
    ho_ref[...] = _rms(xn, g_ref[...]).astype(ho_ref.dtype)


def _combine(x, info, ys, pos_flat, next_g, h_dtype, tm=256):
    n, d = x.shape
    grid_spec = pltpu.PrefetchScalarGridSpec(
        num_scalar_prefetch=1,
        grid=(n // tm,),
        in_specs=[pl.BlockSpec((tm, d), lambda i, pos: (i, 0)),
                  pl.BlockSpec((tm, LANES), lambda i, pos: (i, 0)),
                  pl.BlockSpec((1, d), lambda i, pos: (0, 0)),
                  pl.BlockSpec(memory_space=pl.ANY)],
        out_specs=[pl.BlockSpec((tm, d), lambda i, pos: (i, 0)),
                   pl.BlockSpec((tm, d), lambda i, pos: (i, 0))],
        scratch_shapes=[pltpu.VMEM((2, 2, tm, d), F32), pltpu.SemaphoreType.DMA((2, 2))],
    )
    return pl.pallas_call(
        functools.partial(_combine_body, tm=tm),
        grid_spec=grid_spec,
        out_shape=[jax.ShapeDtypeStruct((n, d), F32), jax.ShapeDtypeStruct((n, d), h_dtype)],
        compiler_params=_params(("arbitrary",), 32),
        name="moe_combine",
    )(pos_flat, x, info, next_g.reshape(1, d), ys)


def _moe(x, fo, go, w_out, ln_g, router_w, w1, w3, w2, moe_layer, next_g, h_dtype):
    n, d = x.shape
    router_w_pad = jnp.pad(router_w, ((0, 0), (0, LANES - N_EXPERTS)))
    x, info, cnt = _router(x, fo, go, w_out, ln_g, router_w_pad)

    counts = cnt[0, :N_EXPERTS].astype(jnp.int32)
    padded = ((counts + MOE_BLOCK - 1) // MOE_BLOCK) * MOE_BLOCK
    pend = jnp.cumsum(padded)
    pstart = pend - padded
    experts = info[:, R_E0:R_E1 + 1].astype(jnp.int32)
    ranks = info[:, R_RANK0:R_RANK1 + 1].astype(jnp.int32)
    pos_flat = (pstart[experts] + ranks).reshape(-1)
    n_blocks = -(-(2 * n) // MOE_BLOCK) + N_EXPERTS
    block_e = jnp.clip(jnp.searchsorted(pend, jnp.arange(n_blocks, dtype=jnp.int32) * MOE_BLOCK,
                                        side="right"), 0, N_EXPERTS - 1).astype(jnp.int32)
    n_used = (pend[-1:] // MOE_BLOCK).astype(jnp.int32)

    xs = _dispatch(x, ln_g, pos_flat, n_blocks * MOE_BLOCK)
    ys = _experts(xs, block_e, n_used, w1, w3, w2, moe_layer)
    return _combine(x, info, ys, pos_flat, next_g, h_dtype)


def _in_weights(w_in):
    o = 3 * FOX_W
    ff = w_in[:, o:o + FOX_HEADS]
    o += FOX_HEADS
    gqkv = w_in[:, o:o + REST_QKV]
    o += REST_QKV
    gab = w_in[:, o:o + 2 * GDN_HEADS]
    o += 2 * GDN_HEADS
    gz = w_in[:, o:o + GDN_W]
    fox = w_in[:, :3 * FOX_W]
    fox = jnp.concatenate([fox[:, :FOX_W] * (LOG2E * FOX_HEAD_DIM ** -0.5), fox[:, FOX_W:]], axis=1)
    pad = jnp.zeros((w_in.shape[0], LANES - G_ROWS), w_in.dtype)
    return jnp.concatenate([fox, gqkv, gz, ff, gab, pad], axis=1).astype(BF16)


def _gate_params(f_bias, dt_bias, a_log):
    row0 = jnp.zeros((LANES,), F32).at[G_FOX:G_FOX + FOX_HEADS].set(f_bias)
    row0 = row0.at[G_DEC:G_DEC + GDN_HEADS].set(dt_bias)
    row1 = jnp.zeros((LANES,), F32).at[G_DEC:G_DEC + GDN_HEADS].set(a_log)
    return jnp.zeros((8, LANES), F32).at[0].set(row0).at[1].set(row1)


def kernel(x, ln1_g, w_in, fox_f_bias, fox_norm_g, gdn_conv_w, gdn_a_log, gdn_dt_bias, gdn_norm_g,
           w_out, ln2_g, ffn_w1, ffn_w3, ffn_w2, router_w, exp_w1, exp_w3, exp_w2, final_g):
    b, t, d = x.shape
    n = b * t
    depth = w_in.shape[0]
    xr = x.reshape(n, d)
    ew1, ew3, ew2 = exp_w1.astype(BF16), exp_w3.astype(BF16), exp_w2.astype(BF16)
    h = _rmsnorm_rows(xr, ln1_g[0], BF16)
    for layer in range(depth):
        fox2, rest2 = _inproj(h, _in_weights(w_in[layer]))
        fox3 = fox2.reshape(b, t, 3 * FOX_W)
        rest3 = rest2.reshape(b, t, REST_W)
        gates3, gates_t, k_aug = _gates(rest3, _gate_params(fox_f_bias[layer], gdn_dt_bias[layer],
                                                            gdn_a_log[layer]))
        fo = _fox(fox3, gates_t, k_aug, fox_norm_g[layer])
        go = _gdn(rest3, gates3, gates_t, gdn_conv_w[layer], gdn_norm_g[layer])
        fo2, go2 = fo.reshape(n, FOX_W), go.reshape(n, GDN_W)
        wo = w_out[layer].astype(BF16)
        last = layer == depth - 1
        next_g = final_g if last else ln1_g[layer + 1]
        h_dtype = F32 if last else BF16
        j = layer // 2
        if layer % 2 == 0:
            xr, h = _ffn(xr, fo2, go2, wo, ln2_g[layer], ffn_w1[j].astype(BF16),
                         ffn_w3[j].astype(BF16), ffn_w2[j].astype(BF16), next_g, h_dtype)
        else:
            xr, h = _moe(xr, fo2, go2, wo, ln2_g[layer], router_w[j], ew1, ew3, ew2, j, next_g,
                         h_dtype)
    return h.reshape(b, t, d)
```

```python
import functools

import jax
import jax.numpy as jnp
from jax import lax
from jax.experimental import pallas as pl
from jax.experimental.pallas import tpu as pltpu

F32 = jnp.float32
BF16 = jnp.bfloat16

D_MODEL = 1024
FOX_HEADS = 8
FOX_HEAD_DIM = 64
FOX_W = FOX_HEADS * FOX_HEAD_DIM
GDN_HEADS = 4
GDN_DK = 128
GDN_DV = 128
GDN_W = GDN_HEADS * GDN_DK
CONV_K = 4
GDN_CHUNK = 64
N_EXPERTS = 8
MOE_BLOCK = 512
EPS = 1e-6

LANES = 128
NEG_BIG = -1e30
LOG2E = 1.4426950408889634
MIB = 1024 * 1024

REST_QKV = 3 * GDN_W
REST_Z = REST_QKV
REST_GATE = REST_QKV + GDN_W
REST_W = REST_GATE + LANES
GATE_BLK = REST_GATE // LANES
G_FOX = 0
G_DEC = FOX_HEADS
G_BETA = FOX_HEADS + GDN_HEADS
G_ROWS = FOX_HEADS + 2 * GDN_HEADS


def _dot(a, b, precision=None):
    return jnp.dot(a, b, preferred_element_type=F32, precision=precision)


def _dot_nt(a, b, precision=None):
    return lax.dot_general(a, b, (((1,), (1,)), ((), ())),
                           preferred_element_type=F32, precision=precision)


def _dot_tn(a, b, precision=None):
    return lax.dot_general(a, b, (((0,), (0,)), ((), ())),
                           preferred_element_type=F32, precision=precision)


def _params(semantics, vmem_mib):
    return pltpu.CompilerParams(dimension_semantics=semantics,
                                vmem_limit_bytes=vmem_mib * MIB)


def _rms(x, g):
    return x * lax.rsqrt(jnp.mean(x * x, axis=-1, keepdims=True) + EPS) * g


def _silu(x):
    return x * jax.nn.sigmoid(x)


def _resident(shape):
    nd = len(shape)
    return pl.BlockSpec(shape, lambda *_: (0,) * nd, pipeline_mode=pl.Buffered(1))


def _rms_body(x_ref, g_ref, o_ref):
    o_ref[...] = _rms(x_ref[...], g_ref[...]).astype(o_ref.dtype)


def _rmsnorm_rows(x, g, out_dtype, tm=1024):
    n, d = x.shape
    return pl.pallas_call(
        _rms_body,
        grid=(n // tm,),
        in_specs=[pl.BlockSpec((tm, d), lambda i: (i, 0)),
                  pl.BlockSpec((1, d), lambda i: (0, 0))],
        out_specs=pl.BlockSpec((tm, d), lambda i: (i, 0)),
        out_shape=jax.ShapeDtypeStruct((n, d), out_dtype),
        compiler_params=_params(("parallel",), 32),
        name="rmsnorm",
    )(x, g.reshape(1, d))


def _inproj_body(h_ref, w_ref, ofox_ref, orest_ref, *, col_chunk):
    h = h_ref[...]
    nf = ofox_ref.shape[1]
    nr = orest_ref.shape[1]
    for c in range(0, nf, col_chunk):
        ofox_ref[:, c:c + col_chunk] = _dot(h, w_ref[:, c:c + col_chunk]).astype(ofox_ref.dtype)
    for c in range(0, nr, col_chunk):
        e = min(c + col_chunk, nr)
        orest_ref[:, c:e] = _dot(h, w_ref[:, nf + c:nf + e])


def _inproj(h, w_all, tm=512):
    n, d = h.shape
    nf, nr = 3 * FOX_W, REST_W
    return pl.pallas_call(
        functools.partial(_inproj_body, col_chunk=512),
        grid=(n // tm,),
        in_specs=[pl.BlockSpec((tm, d), lambda i: (i, 0)),
                  _resident((d, nf + nr))],
        out_specs=[pl.BlockSpec((tm, nf), lambda i: (i, 0)),
                   pl.BlockSpec((tm, nr), lambda i: (i, 0))],
        out_shape=[jax.ShapeDtypeStruct((n, nf), BF16),
                   jax.ShapeDtypeStruct((n, nr), F32)],
        compiler_params=_params(("parallel",), 40),
        name="inproj",
    )(h, w_all)


def _gates_body(z_ref, p_ref, o_ref, ot_ref, ka_ref, carry_ref, *, tt):
    @pl.when(pl.program_id(1) == 0)
    def _():
        carry_ref[...] = jnp.zeros_like(carry_ref)

    z = z_ref[0] + p_ref[0:1, :]
    lane = lax.broadcasted_iota(jnp.int32, z.shape, 1)
    tail = jnp.log(1.0 + jnp.exp(-jnp.abs(z)))
    log_sig = jnp.minimum(z, 0.0) - tail
    softplus = jnp.maximum(z, 0.0) + tail
    decay = -jnp.exp(p_ref[1:2, :]) * softplus
    val = jnp.where(lane < G_DEC, log_sig, jnp.where(lane < G_BETA, decay, jax.nn.sigmoid(z)))

    ri = lax.broadcasted_iota(jnp.int32, (tt, tt), 0)
    ci = lax.broadcasted_iota(jnp.int32, (tt, tt), 1)
    tri = jnp.where(ci <= ri, 1.0, 0.0)
    blk = jnp.where((ci <= ri) & (ri // GDN_CHUNK == ci // GDN_CHUNK), 1.0, 0.0)
    both = jnp.concatenate([tri, blk], axis=0).astype(BF16)
    v_hi = val.astype(BF16)
    v_mid = (val - v_hi.astype(F32)).astype(BF16)
    v_lo = (val - v_hi.astype(F32) - v_mid.astype(F32)).astype(BF16)
    hi_mid = _dot(both, jnp.concatenate([v_hi, v_mid], axis=1))
    cums = hi_mid[:, :LANES] + hi_mid[:, LANES:] + _dot(both, v_lo)
    full_cum = cums[:tt] + carry_ref[...]
    chunk_cum = cums[tt:]
    out = jnp.where(lane < G_DEC, full_cum, jnp.where(lane < G_BETA, chunk_cum, val))
    carry_ref[...] = full_cum[tt - 1:tt, :]
    o_ref[0] = out
    ot_ref[0] = out.T[:G_ROWS, :]

    neg_c = jnp.where(lane < G_DEC, -LOG2E * full_cum, 0.0)
    hi = neg_c.astype(BF16).astype(F32)
    mid = (neg_c - hi).astype(BF16).astype(F32)
    lo = (neg_c - hi - mid).astype(BF16).astype(F32)
    aug = hi + pltpu.roll(mid, FOX_HEADS, axis=1) + pltpu.roll(lo, 2 * FOX_HEADS, axis=1)
    ka_ref[0] = aug.astype(BF16)


def _gates(rest3, gate_params, tt=512):
    b, t, _ = rest3.shape
    return pl.pallas_call(
        functools.partial(_gates_body, tt=tt),
        grid=(b, t // tt),
        in_specs=[pl.BlockSpec((1, tt, LANES), lambda i, j: (i, j, GATE_BLK)),
                  pl.BlockSpec((8, LANES), lambda i, j: (0, 0))],
        out_specs=[pl.BlockSpec((1, tt, LANES), lambda i, j: (i, j, 0)),
                   pl.BlockSpec((1, G_ROWS, tt), lambda i, j: (i, 0, j)),
                   pl.BlockSpec((1, tt, LANES), lambda i, j: (i, j, 0))],
        out_shape=[jax.ShapeDtypeStruct((b, t, LANES), F32),
                   jax.ShapeDtypeStruct((b, G_ROWS, t), F32),
                   jax.ShapeDtypeStruct((b, t, LANES), BF16)],
        scratch_shapes=[pltpu.VMEM((1, LANES), F32)],
        compiler_params=_params(("parallel", "arbitrary"), 32),
        name="gates",
    )(rest3, gate_params)


def _fox_body(q_ref, k_ref, ka_ref, v_ref, c_ref, g_ref, o_ref, vt_ref, acc_ref, s_ref, cm_ref, m_ref,
              *, tq):
    hp = pl.program_id(1)
    qi = pl.program_id(2)
    half = FOX_HEAD_DIM
    t_total = k_ref.shape[1]
    lane_row = lax.broadcasted_iota(jnp.int32, (1, LANES), 1)
    sub = lax.broadcasted_iota(jnp.int32, (LANES, tq), 0)

    @pl.when(qi == 0)
    def _():
        for c in range(t_total // tq):
            vt = v_ref[0, c * tq:(c + 1) * tq, :].astype(F32).T
            cols = slice(c * tq, (c + 1) * tq)
            vt_ref[0, :, cols] = jnp.where(sub < half, vt, jnp.where(sub == half, 1.0, 0.0)).astype(BF16)
            vt_ref[1, :, cols] = jnp.where(sub >= half, vt, jnp.where(sub == 0, 1.0, 0.0)).astype(BF16)

    q = q_ref[0]
    zero = jnp.zeros_like(q)
    qs = pl.multiple_of(qi * tq, tq)
    q_aug, cq = [], []
    for j in range(2):
        head = 2 * hp + j
        pick = (lane_row == head) | (lane_row == FOX_HEADS + head) | (lane_row == 2 * FOX_HEADS + head)
        ones = jnp.broadcast_to(jnp.where(pick, 1.0, 0.0).astype(BF16), (tq, LANES))
        qj = jnp.where(lane_row < half, q, zero) if j == 0 else jnp.where(lane_row < half, zero, q)
        q_aug.append(jnp.concatenate([qj, ones], axis=1).astype(F32).T.astype(BF16))
        cq.append(c_ref[0, 0, j:j + 1, pl.ds(qs, tq)] * LOG2E)

    acc_ref[...] = jnp.zeros_like(acc_ref)
    m_ref[...] = jnp.full(m_ref.shape, NEG_BIG, F32)

    def scores(ki, j, diagonal):
        ks = pl.multiple_of(ki * tq, tq)
        k_aug = jnp.concatenate([k_ref[0, pl.ds(ks, tq), :], ka_ref[0, pl.ds(ks, tq), :]], axis=1)
        s = _dot(k_aug, q_aug[j])
        if diagonal:
            ri = lax.broadcasted_iota(jnp.int32, s.shape, 0)
            ci = lax.broadcasted_iota(jnp.int32, s.shape, 1)
            s = jnp.where(ri <= ci, s, NEG_BIG)
        return s

    def stash(j, s):
        s_ref[j] = s
        cm_ref[j] = jnp.max(s, axis=0, keepdims=True)

    def consume(ki, j):
        ks = pl.multiple_of(ki * tq, tq)
        m_old = m_ref[j]
        m_new = jnp.maximum(m_old, cq[j] + cm_ref[j])
        p = jnp.exp2(s_ref[j] + (cq[j] - m_new)).astype(BF16)
        acc_ref[j] = jnp.exp2(m_old - m_new) * acc_ref[j] + _dot(vt_ref[j, :, pl.ds(ks, tq)], p)
        m_ref[j] = m_new

    def advance(ki_next, diagonal):
        for j in range(2):
            s_next = scores(ki_next, j, diagonal)
            consume(ki_next - 1, j)
            stash(j, s_next)

    @pl.when(qi == 0)
    def _():
        for j in range(2):
            stash(j, scores(0, j, True))

    @pl.when(qi > 0)
    def _():
        for j in range(2):
            stash(j, scores(0, j, False))

    def steady(ki, carry):
        advance(ki + 1, False)
        return carry

    lax.fori_loop(0, qi - 1, steady, 0)

    @pl.when(qi > 0)
    def _():
        advance(qi, True)

    for j in range(2):
        consume(qi, j)

    a0 = acc_ref[0]
    a1 = acc_ref[1]
    ot = jnp.where(sub < half, a0 / a0[half:half + 1, :], a1 / a1[0:1, :])
    o = ot.T
    lo = lane_row < half
    sq = o * o
    ms0 = jnp.sum(jnp.where(lo, sq, 0.0), axis=1, keepdims=True) / half
    ms1 = jnp.sum(jnp.where(lo, 0.0, sq), axis=1, keepdims=True) / half
    inv = lax.rsqrt(jnp.where(lo, ms0, ms1) + EPS)
    o_ref[0] = (o * inv * g_ref[...]).astype(o_ref.dtype)


def _fox(fox3, gates_t, k_aug, norm_g, tq=512):
    b, t, _ = fox3.shape
    npair = FOX_HEADS // 2
    c4 = gates_t.reshape(b, G_ROWS // 2, 2, t)
    return pl.pallas_call(
        functools.partial(_fox_body, tq=tq),
        grid=(b, npair, t // tq),
        in_specs=[pl.BlockSpec((1, tq, LANES), lambda i, p, j: (i, j, p)),
                  pl.BlockSpec((1, t, LANES), lambda i, p, j: (i, 0, npair + p)),
                  pl.BlockSpec((1, t, LANES), lambda i, p, j: (i, 0, 0)),
                  pl.BlockSpec((1, t, LANES), lambda i, p, j: (i, 0, 2 * npair + p)),
                  pl.BlockSpec((1, 1, 2, t), lambda i, p, j: (i, p, 0, 0)),
                  pl.BlockSpec((1, LANES), lambda i, p, j: (0, p))],
        out_specs=pl.BlockSpec((1, tq, LANES), lambda i, p, j: (i, j, p)),
        out_shape=jax.ShapeDtypeStruct((b, t, FOX_W), BF16),
        scratch_shapes=[pltpu.VMEM((2, LANES, t), BF16),
                        pltpu.VMEM((2, LANES, tq), F32),
                        pltpu.VMEM((2, tq, tq), F32),
                        pltpu.VMEM((2, 1, tq), F32),
                        pltpu.VMEM((2, 1, tq), F32)],
        compiler_params=_params(("parallel", "parallel", "arbitrary"), 48),
        name="fox_attention",
    )(fox3, fox3, k_aug, fox3, c4, norm_g.reshape(1, FOX_W))


SUPER = 256
GDN_HEADS_PER_STEP = 4
HALO = 8


def _split_bf16(x):
    hi = x.astype(BF16)
    return hi, (x - hi.astype(F32)).astype(BF16)


def _dot_split(a, b, dot=_dot):
    ah, al = a
    bh, bl = b
    return dot(ah, bh) + dot(ah, bl) + dot(al, bh)


def _gdn_body(xq_ref, xk_ref, xv_ref, z_ref, gc_ref, gr_ref, wq_ref, wk_ref, wv_ref, ng_ref,
              o_ref, s_ref, halo_ref, buf_ref, obuf_ref, *, tt):
    hps = GDN_HEADS_PER_STEP
    pair = pl.program_id(1)

    @pl.when(pl.program_id(2) == 0)
    def _():
        s_ref[...] = jnp.zeros_like(s_ref)
        halo_ref[...] = jnp.zeros_like(halo_ref)

    def conv_silu(x_ref, hh, idx, w_ref):
        lanes = slice(hh * LANES, (hh + 1) * LANES)
        slot = hh * 3 + idx
        x = x_ref[0, :, lanes]
        buf_ref[slot, 0:HALO, :] = halo_ref[slot]
        buf_ref[slot, HALO:HALO + tt, :] = x
        halo_ref[slot] = x[tt - HALO:tt, :]
        y = jnp.zeros_like(x)
        for j in range(CONV_K):
            off = HALO - (CONV_K - 1) + j
            y = y + w_ref[j:j + 1, lanes] * buf_ref[slot, off:off + tt, :]
        return _silu(y)

    gates = gc_ref[0]
    lane_t = lax.broadcasted_iota(jnp.int32, gates.shape, 1)
    ri = lax.broadcasted_iota(jnp.int32, (SUPER, SUPER), 0)
    ci = lax.broadcasted_iota(jnp.int32, (SUPER, SUPER), 1)
    same = (ri // GDN_CHUNK) == (ci // GDN_CHUNK)
    tril = same & (ci <= ri)
    strict = same & (ci < ri)
    eye = jnp.where(ri == ci, 1.0, 0.0).astype(F32)

    heads = []
    for hh in range(hps):
        head = hps * pair + hh
        q = conv_silu(xq_ref, hh, 0, wq_ref)
        k = conv_silu(xk_ref, hh, 1, wk_ref)
        v = conv_silu(xv_ref, hh, 2, wv_ref)
        q = q * lax.rsqrt(jnp.sum(q * q, axis=-1, keepdims=True) + EPS) * (GDN_DK ** -0.5)
        k = k * lax.rsqrt(jnp.sum(k * k, axis=-1, keepdims=True) + EPS)
        gcol = jnp.sum(jnp.where(lane_t == G_DEC + head, gates, 0.0), axis=1, keepdims=True)
        bcol = jnp.sum(jnp.where(lane_t == G_BETA + head, gates, 0.0), axis=1, keepdims=True)
        eg = jnp.exp(gcol)
        kb = k * bcol
        heads.append(dict(hh=hh, k=k, gcol=gcol, grow=gr_ref[0, hh], kb_split=_split_bf16(kb),
                          k_split=_split_bf16(k), k16=k.astype(BF16), q16=q.astype(BF16),
                          rhs16=jnp.concatenate([v * bcol, kb * eg], axis=1).astype(BF16),
                          qe=q * eg))

    blocks = []
    for hd in heads:
        for sc in range(tt // SUPER):
            rows = slice(sc * SUPER, (sc + 1) * SUPER)
            diff = hd["gcol"][rows] - hd["grow"][:, rows]
            decay = jnp.where(tril, jnp.exp(jnp.where(tril, diff, 0.0)), 0.0)
            gram = _dot_split(tuple(x[rows] for x in hd["kb_split"]),
                              tuple(x[rows] for x in hd["k_split"]), _dot_nt)
            a = jnp.where(strict, gram * decay, 0.0)
            blocks.append(dict(hd=hd, rows=rows, decay=decay, inv=eye - a, pw=a.astype(BF16)))

    for _ in range(5):
        for blk in blocks:
            blk["pw"] = _dot(blk["pw"], blk["pw"]).astype(BF16)
        for blk in blocks:
            blk["inv"] = blk["inv"] + _dot(blk["inv"].astype(BF16), blk["pw"])

    chunks = SUPER // GDN_CHUNK
    for blk in blocks:
        hd, rows = blk["hd"], blk["rows"]
        inv_hi, inv_lo = _split_bf16(blk["inv"])
        uw16 = (_dot(inv_hi, hd["rhs16"][rows]) + _dot(inv_lo, hd["rhs16"][rows])).astype(BF16)
        intra = jnp.where(tril, _dot_nt(hd["q16"][rows], hd["k16"][rows]) * blk["decay"], 0.0)
        iuw = _dot(intra.astype(BF16), uw16)
        blk["o_fixed"] = iuw[:, :GDN_DV]
        blk["q_eff"] = (hd["qe"][rows] - iuw[:, GDN_DV:]).astype(BF16)
        blk["s_decay"], blk["s_add"], blk["s_mix"] = [], [], []
        for c in range(chunks):
            lr = slice(c * GDN_CHUNK, (c + 1) * GDN_CHUNK)
            gr = slice(rows.start + lr.start, rows.start + lr.stop)
            g_last = hd["gcol"][gr.stop - 1:gr.stop, :]
            k_dec = (hd["k"][gr] * jnp.exp(g_last - hd["gcol"][gr])).astype(BF16)
            kuw = _dot_tn(k_dec, uw16[lr])
            blk["s_decay"].append(jnp.exp(g_last))
            blk["s_add"].append(kuw[:, :GDN_DV])
            blk["s_mix"].append(kuw[:, GDN_DV:].astype(BF16))

    for sc in range(tt // SUPER):
        for c in range(chunks):
            lr = slice(c * GDN_CHUNK, (c + 1) * GDN_CHUNK)
            gr = slice(sc * SUPER + lr.start, sc * SUPER + lr.stop)
            for blk in blocks:
                if blk["rows"].start != sc * SUPER:
                    continue
                hh = blk["hd"]["hh"]
                s = s_ref[hh]
                s16 = s.astype(BF16)
                s_ref[hh] = s * blk["s_decay"][c] + blk["s_add"][c] - _dot(blk["s_mix"][c], s16)
                obuf_ref[hh, gr, :] = blk["o_fixed"][lr] + _dot(blk["q_eff"][lr], s16)

    for hh in range(hps):
        lanes = slice(hh * LANES, (hh + 1) * LANES)
        o = _rms(obuf_ref[hh], ng_ref[...]) * _silu(z_ref[0, :, lanes])
        o_ref[0, :, lanes] = o.astype(o_ref.dtype)


def _gdn(rest3, gates3, gates_t, conv_w, norm_g, tt=512):
    b, t, _ = rest3.shape
    hps = GDN_HEADS_PER_STEP
    npair = GDN_HEADS // hps
    wide = hps * LANES
    gr4 = gates_t.reshape(b, G_ROWS, 1, t)
    x_spec = lambda off: pl.BlockSpec((1, tt, wide), lambda i, p, j: (i, j, off + p))
    w_spec = lambda off: pl.BlockSpec((CONV_K, wide), lambda i, p, j: (0, off + p))
    return pl.pallas_call(
        functools.partial(_gdn_body, tt=tt),
        grid=(b, npair, t // tt),
        in_specs=[x_spec(0), x_spec(npair), x_spec(2 * npair), x_spec(3 * npair),
                  pl.BlockSpec((1, tt, LANES), lambda i, p, j: (i, j, 0)),
                  pl.BlockSpec((1, hps, 1, tt), lambda i, p, j: (i, G_DEC // hps + p, 0, j)),
                  w_spec(0), w_spec(npair), w_spec(2 * npair),
                  pl.BlockSpec((1, LANES), lambda i, p, j: (0, 0))],
        out_specs=pl.BlockSpec((1, tt, wide), lambda i, p, j: (i, j, p)),
        out_shape=jax.ShapeDtypeStruct((b, t, GDN_W), BF16),
        scratch_shapes=[pltpu.VMEM((hps, GDN_DK, GDN_DV), F32),
                        pltpu.VMEM((hps * 3, HALO, LANES), F32),
                        pltpu.VMEM((hps * 3, tt + HALO, LANES), F32),
                        pltpu.VMEM((hps, tt, LANES), F32)],
        compiler_params=_params(("parallel", "parallel", "arbitrary"), 48),
        name="gated_delta",
    )(rest3, rest3, rest3, rest3, gates3, gr4, conv_w, conv_w, conv_w, norm_g.reshape(1, GDN_DV))


def _mixer_residual(x_ref, fo_ref, go_ref, wo_ref):
    y = _dot(fo_ref[...], wo_ref[0:FOX_W, :]) + _dot(go_ref[...], wo_ref[FOX_W:FOX_W + GDN_W, :])
    return x_ref[...] + y


def _ffn_body(x_ref, fo_ref, go_ref, wo_ref, g2_ref, w1_ref, w3_ref, w2_ref, g_ref,
              xo_ref, ho_ref, acc_ref, *, f_chunk):
    xm = _mixer_residual(x_ref, fo_ref, go_ref, wo_ref)
    h = _rms(xm, g2_ref[...]).astype(BF16)
    d_ff = w1_ref.shape[1]
    for i, c in enumerate(range(0, d_ff, f_chunk)):
        a = _dot(h, w1_ref[:, c:c + f_chunk])
        b = _dot(h, w3_ref[:, c:c + f_chunk])
        y = _dot((_silu(a) * b).astype(BF16), w2_ref[c:c + f_chunk, :])
        if i == 0:
            acc_ref[...] = y
        else:
            acc_ref[...] += y
    xn = xm + acc_ref[...]
    xo_ref[...] = xn
    ho_ref[...] = _rms(xn, g_ref[...]).astype(ho_ref.dtype)


def _ffn(x, fo, go, w_out, ln2_g, w1, w3, w2, next_g, h_dtype, tm=512, f_chunk=256):
    n, d = x.shape
    d_ff = w1.shape[1]
    row = lambda w: pl.BlockSpec((tm, w), lambda i: (i, 0))
    vec = pl.BlockSpec((1, d), lambda i: (0, 0))
    return pl.pallas_call(
        functools.partial(_ffn_body, f_chunk=f_chunk),
        grid=(n // tm,),
        in_specs=[row(d), row(FOX_W), row(GDN_W), _resident((FOX_W + GDN_W, d)), vec,
                  _resident((d, d_ff)), _resident((d, d_ff)), _resident((d_ff, d)), vec],
        out_specs=[row(d), row(d)],
        out_shape=[jax.ShapeDtypeStruct((n, d), F32), jax.ShapeDtypeStruct((n, d), h_dtype)],
        scratch_shapes=[pltpu.VMEM((tm, d), F32)],
        compiler_params=_params(("parallel",), 54),
        name="outproj_dense_swiglu",
    )(x, fo, go, w_out, ln2_g.reshape(1, d), w1, w3, w2, next_g.reshape(1, d))


R_E0, R_E1, R_W0, R_W1, R_RANK0, R_RANK1 = 0, 1, 2, 3, 4, 5


def _router_body(x_ref, fo_ref, go_ref, wo_ref, g_ref, w_ref, xo_ref, info_ref, cnt_ref, carry_ref,
                 *, tm):
    @pl.when(pl.program_id(0) == 0)
    def _():
        carry_ref[...] = jnp.zeros_like(carry_ref)

    xm = _mixer_residual(x_ref, fo_ref, go_ref, wo_ref)
    xo_ref[...] = xm
    h = _rms(xm, g_ref[...])
    h_hi, h_lo = _split_bf16(h)
    w_hi, w_lo = _split_bf16(w_ref[...])
    two = _dot(h_hi, jnp.concatenate([w_hi, w_lo], axis=1))
    logits = two[:, :LANES] + two[:, LANES:] + _dot(h_lo, w_hi)
    lane = lax.broadcasted_iota(jnp.int32, logits.shape, 1)
    valid = lane < N_EXPERTS
    lm = jnp.where(valid, logits, NEG_BIG)
    ex = jnp.exp(lm - jnp.max(lm, axis=1, keepdims=True))
    probs = jnp.where(valid, ex / jnp.sum(ex, axis=1, keepdims=True), -1.0)
    p0 = jnp.max(probs, axis=1, keepdims=True)
    e0 = jnp.min(jnp.where(probs == p0, lane, LANES), axis=1, keepdims=True)
    rest = jnp.where(lane == e0, -1.0, probs)
    p1 = jnp.max(rest, axis=1, keepdims=True)
    e1 = jnp.min(jnp.where(rest == p1, lane, LANES), axis=1, keepdims=True)
    denom = p0 + p1

    picked = jnp.where((lane == e0) | (lane == e1), 1.0, 0.0).astype(F32)
    ri = lax.broadcasted_iota(jnp.int32, (tm, tm), 0)
    ci = lax.broadcasted_iota(jnp.int32, (tm, tm), 1)
    before = jnp.where(ci < ri, 1.0, 0.0).astype(BF16)
    earlier = _dot(before, picked.astype(BF16)) + carry_ref[...]
    rank0 = jnp.sum(jnp.where(lane == e0, earlier, 0.0), axis=1, keepdims=True)
    rank1 = jnp.sum(jnp.where(lane == e1, earlier, 0.0), axis=1, keepdims=True)
    carry_ref[...] += jnp.sum(picked, axis=0, keepdims=True)

    info = jnp.zeros(logits.shape, F32)
    for col, val in ((R_E0, e0.astype(F32)), (R_E1, e1.astype(F32)), (R_W0, p0 / denom),
                     (R_W1, p1 / denom), (R_RANK0, rank0), (R_RANK1, rank1)):
        info = jnp.where(lane == col, val, info)
    info_ref[...] = info
    cnt_ref[...] = jnp.broadcast_to(carry_ref[...], cnt_ref.shape)


def _router(x, fo, go, w_out, ln_g, router_w_pad, tm=512):
    n, d = x.shape
    row = lambda w: pl.BlockSpec((tm, w), lambda i: (i, 0))
    return pl.pallas_call(
        functools.partial(_router_body, tm=tm),
        grid=(n // tm,),
        in_specs=[row(d), row(FOX_W), row(GDN_W), _resident((FOX_W + GDN_W, d)),
                  pl.BlockSpec((1, d), lambda i: (0, 0)),
                  pl.BlockSpec((d, LANES), lambda i: (0, 0))],
        out_specs=[row(d), row(LANES),
                   pl.BlockSpec((8, LANES), lambda i: (0, 0))],
        out_shape=[jax.ShapeDtypeStruct((n, d), F32),
                   jax.ShapeDtypeStruct((n, LANES), F32),
                   jax.ShapeDtypeStruct((8, LANES), F32)],
        scratch_shapes=[pltpu.VMEM((1, LANES), F32)],
        compiler_params=_params(("arbitrary",), 32),
        name="outproj_moe_router",
    )(x, fo, go, w_out, ln_g.reshape(1, d), router_w_pad)


def _row_copy(src_ref, src_row, dst_ref, dst_row, sem):
    return pltpu.make_async_copy(src_ref.at[pl.ds(src_row, 1)], dst_ref.at[pl.ds(dst_row, 1)], sem)


def _dispatch_body(pos_ref, tail_ref, x_ref, g_ref, xs_ref, buf_ref, sem_ref, zero_sem_ref, *, tm):
    i = pl.program_id(0)
    slot = i % 2

    @pl.when(i == 0)
    def _():
        buf_ref[1] = jnp.zeros(buf_ref.shape[1:], F32)
        first_spare = xs_ref.shape[0] - N_EXPERTS * MOE_BLOCK

        def fill(block_row, part):
            start = pl.multiple_of(block_row + part * tm, tm)
            return pltpu.make_async_copy(buf_ref.at[1], xs_ref.at[pl.ds(start, tm)], zero_sem_ref)

        parts = range(MOE_BLOCK // tm)
        for e in range(N_EXPERTS):
            @pl.when(tail_ref[e] >= 0)
            def _():
                for part in parts:
                    fill(tail_ref[e], part).start()
        for e in range(N_EXPERTS):
            @pl.when(tail_ref[e] >= 0)
            def _():
                for part in parts:
                    fill(tail_ref[e], part).wait()
        spare = [fill(first_spare + e * MOE_BLOCK, part) for e in range(N_EXPERTS) for part in parts]
        for cp in spare:
            cp.start()
        for cp in spare:
            cp.wait()

    buf_ref[slot] = _rms(x_ref[...], g_ref[...])

    def issue(r, carry):
        base = 2 * (i * tm + r)
        for j in range(2):
            _row_copy(buf_ref.at[slot], r, xs_ref, pos_ref[base + j], sem_ref.at[slot]).start()
        return carry

    lax.fori_loop(0, tm, issue, 0, unroll=8)

    def wait_slot(s):
        for _ in range(2):
            pltpu.make_async_copy(buf_ref.at[s], xs_ref.at[pl.ds(0, tm)], sem_ref.at[s]).wait()

    @pl.when(i > 0)
    def _():
        wait_slot(1 - slot)

    @pl.when(i == pl.num_programs(0) - 1)
    def _():
        wait_slot(slot)


def _dispatch(x, ln_g, pos_flat, last_block_row, n_slots, tm=256):
    n, d = x.shape
    grid_spec = pltpu.PrefetchScalarGridSpec(
        num_scalar_prefetch=2,
        grid=(n // tm,),
        in_specs=[pl.BlockSpec((tm, d), lambda i, pos, tail: (i, 0)),
                  pl.BlockSpec((1, d), lambda i, pos, tail: (0, 0))],
        out_specs=pl.BlockSpec(memory_space=pl.ANY),
        scratch_shapes=[pltpu.VMEM((2, tm, d), F32), pltpu.SemaphoreType.DMA((2,)),
                        pltpu.SemaphoreType.DMA],
    )
    return pl.pallas_call(
        functools.partial(_dispatch_body, tm=tm),
        grid_spec=grid_spec,
        out_shape=jax.ShapeDtypeStruct((n_slots, d), F32),
        compiler_params=_params(("arbitrary",), 32),
        name="moe_dispatch",
    )(pos_flat, last_block_row, x, ln_g.reshape(1, d))


def _experts_body(be_ref, nu_ref, xs_ref, w1_ref, w3_ref, w2_ref, o_ref, xb_ref, *, f_sub):
    del be_ref
    b = pl.program_id(0)
    f = pl.program_id(1)
    used = b < nu_ref[0]

    @pl.when(f == 0)
    def _():
        o_ref[...] = jnp.zeros_like(o_ref)

    @pl.when(used & (f == 0))
    def _():
        xb_ref[...] = xs_ref[...].astype(BF16)

    @pl.when(used)
    def _():
        xb = xb_ref[...]
        for c in range(0, w1_ref.shape[1], f_sub):
            a = _dot(xb, w1_ref[:, c:c + f_sub])
            g = _dot(xb, w3_ref[:, c:c + f_sub])
            o_ref[...] += _dot((_silu(a) * g).astype(BF16), w2_ref[c:c + f_sub, :])


def _experts(xs, block_e, n_used, w1, w3, w2, moe_layer, f_chunk=1792, f_sub=256):
    p, d = xs.shape
    d_ff = w1.shape[3]
    nb = p // MOE_BLOCK
    grid_spec = pltpu.PrefetchScalarGridSpec(
        num_scalar_prefetch=2,
        grid=(nb, d_ff // f_chunk),
        in_specs=[pl.BlockSpec((MOE_BLOCK, d), lambda b, f, be, nu: (jnp.minimum(b, nu[0] - 1), 0)),
                  pl.BlockSpec((None, None, d, f_chunk), lambda b, f, be, nu: (moe_layer, be[b], 0, f)),
                  pl.BlockSpec((None, None, d, f_chunk), lambda b, f, be, nu: (moe_layer, be[b], 0, f)),
                  pl.BlockSpec((None, None, f_chunk, d), lambda b, f, be, nu: (moe_layer, be[b], f, 0))],
        out_specs=pl.BlockSpec((MOE_BLOCK, d), lambda b, f, be, nu: (b, 0)),
        scratch_shapes=[pltpu.VMEM((MOE_BLOCK, d), BF16)],
    )
    return pl.pallas_call(
        functools.partial(_experts_body, f_sub=f_sub),
        grid_spec=grid_spec,
        out_shape=jax.ShapeDtypeStruct((p, d), F32),
        compiler_params=_params(("arbitrary", "arbitrary"), 48),
        name="moe_experts",
    )(block_e, n_used, xs, w1, w3, w2)


def _combine_body(pos_ref, x_ref, info_ref, g_ref, ys_ref, xo_ref, ho_ref, y_ref, sem_ref, *, tm):
    i = pl.program_id(0)
    slot = i % 2

    def gather(step, into):
        def issue(r, carry):
            base = 2 * (step * tm + r)
            for j in range(2):
                _row_copy(ys_ref, pos_ref[base + j], y_ref.at[into, j], r, sem_ref.at[into, j]).start()
            return carry

        lax.fori_loop(0, tm, issue, 0, unroll=8)

    @pl.when(i == 0)
    def _():
        gather(i, slot)

    @pl.when(i + 1 < pl.num_programs(0))
    def _():
        gather(i + 1, 1 - slot)

    for j in range(2):
        pltpu.make_async_copy(ys_ref.at[pl.ds(0, tm)], y_ref.at[slot, j], sem_ref.at[slot, j]).wait()

    info = info_ref[...]
    w0 = info[:, R_W0:R_W0 + 1]
    w1 = info[:, R_W1:R_W1 + 1]
    xn = x_ref[...] + (w0 * y_ref[slot, 0] + w1 * y_ref[slot, 1])
    xo_ref[...] = xn
    ho_ref[...] = _rms(xn, g_ref[...]).astype(ho_ref.dtype)


def _combine(x, info, ys, pos_flat, next_g, h_dtype, tm=256):
    n, d = x.shape
    grid_spec = pltpu.PrefetchScalarGridSpec(
        num_scalar_prefetch=1,
        grid=(n // tm,),
        in_specs=[pl.BlockSpec((tm, d), lambda i, pos: (i, 0)),
                  pl.BlockSpec((tm, LANES), lambda i, pos: (i, 0)),
                  pl.BlockSpec((1, d), lambda i, pos: (0, 0)),
                  pl.BlockSpec(memory_space=pl.ANY)],
        out_specs=[pl.BlockSpec((tm, d), lambda i, pos: (i, 0)),
                   pl.BlockSpec((tm, d), lambda i, pos: (i, 0))],
        scratch_shapes=[pltpu.VMEM((2, 2, tm, d), F32), pltpu.SemaphoreType.DMA((2, 2))],
    )
    return pl.pallas_call(
        functools.partial(_combine_body, tm=tm),
        grid_spec=grid_spec,
        out_shape=[jax.ShapeDtypeStruct((n, d), F32), jax.ShapeDtypeStruct((n, d), h_dtype)],
        compiler_params=_params(("arbitrary",), 32),
        name="moe_combine",
    )(pos_flat, x, info, next_g.reshape(1, d), ys)


def _moe(x, fo, go, w_out, ln_g, router_w, w1, w3, w2, moe_layer, next_g, h_dtype):
    n, d = x.shape
    router_w_pad = jnp.pad(router_w, ((0, 0), (0, LANES - N_EXPERTS)))
    x, info, cnt = _router(x, fo, go, w_out, ln_g, router_w_pad)

    counts = cnt[0, :N_EXPERTS].astype(jnp.int32)
    padded = ((counts + MOE_BLOCK - 1) // MOE_BLOCK) * MOE_BLOCK
    pend = jnp.cumsum(padded)
    pstart = pend - padded
    experts = info[:, R_E0:R_E1 + 1].astype(jnp.int32)
    ranks = info[:, R_RANK0:R_RANK1 + 1].astype(jnp.int32)
    pos_flat = (pstart[experts] + ranks).reshape(-1)
    n_blocks = -(-(2 * n) // MOE_BLOCK) + N_EXPERTS
    block_row = jnp.arange(n_blocks, dtype=jnp.int32) * MOE_BLOCK
    block_e = jnp.minimum(jnp.sum(pend[None, :] <= block_row[:, None], axis=1),
                          N_EXPERTS - 1).astype(jnp.int32)
    n_used = (pend[-1:] // MOE_BLOCK).astype(jnp.int32)
    last_block_row = jnp.where(padded > 0, pend - MOE_BLOCK, -1).astype(jnp.int32)

    xs = _dispatch(x, ln_g, pos_flat, last_block_row, n_blocks * MOE_BLOCK)
    ys = _experts(xs, block_e, n_used, w1, w3, w2, moe_layer)
    return _combine(x, info, ys, pos_flat, next_g, h_dtype)


def _in_weights(w_in):
    o = 3 * FOX_W
    ff = w_in[:, o:o + FOX_HEADS]
    o += FOX_HEADS
    gqkv = w_in[:, o:o + REST_QKV]
    o += REST_QKV
    gab = w_in[:, o:o + 2 * GDN_HEADS]
    o += 2 * GDN_HEADS
    gz = w_in[:, o:o + GDN_W]
    fox = w_in[:, :3 * FOX_W]
    fox = jnp.concatenate([fox[:, :FOX_W] * (LOG2E * FOX_HEAD_DIM ** -0.5), fox[:, FOX_W:]], axis=1)
    pad = jnp.zeros((w_in.shape[0], LANES - G_ROWS), w_in.dtype)
    return jnp.concatenate([fox, gqkv, gz, ff, gab, pad], axis=1).astype(BF16)


def _gate_params(f_bias, dt_bias, a_log):
    row0 = jnp.zeros((LANES,), F32).at[G_FOX:G_FOX + FOX_HEADS].set(f_bias)
    row0 = row0.at[G_DEC:G_DEC + GDN_HEADS].set(dt_bias)
    row1 = jnp.zeros((LANES,), F32).at[G_DEC:G_DEC + GDN_HEADS].set(a_log)
    return jnp.zeros((8, LANES), F32).at[0].set(row0).at[1].set(row1)


def kernel(x, ln1_g, w_in, fox_f_bias, fox_norm_g, gdn_conv_w, gdn_a_log, gdn_dt_bias, gdn_norm_g,
           w_out, ln2_g, ffn_w1, ffn_w3, ffn_w2, router_w, exp_w1, exp_w3, exp_w2, final_g):
    b, t, d = x.shape
    n = b * t
    depth = w_in.shape[0]
    xr = x.reshape(n, d)
    ew1, ew3, ew2 = exp_w1.astype(BF16), exp_w3.astype(BF16), exp_w2.astype(BF16)
    h = _rmsnorm_rows(xr, ln1_g[0], BF16)
    for layer in range(depth):
        fox2, rest2 = _inproj(h, _in_weights(w_in[layer]))
        fox3 = fox2.reshape(b, t, 3 * FOX_W)
        rest3 = rest2.reshape(b, t, REST_W)
        gates3, gates_t, k_aug = _gates(rest3, _gate_params(fox_f_bias[layer], gdn_dt_bias[layer],
                                                            gdn_a_log[layer]))
        fo = _fox(fox3, gates_t, k_aug, fox_norm_g[layer])
        go = _gdn(rest3, gates3, gates_t, gdn_conv_w[layer], gdn_norm_g[layer])
        fo2, go2 = fo.reshape(n, FOX_W), go.reshape(n, GDN_W)
        wo = w_out[layer].astype(BF16)
        last = layer == depth - 1
        next_g = final_g if last else ln1_g[layer + 1]
        h_dtype = F32 if last else BF16
        j = layer // 2
        if layer % 2 == 0:
            xr, h = _ffn(xr, fo2, go2, wo, ln2_g[layer], ffn_w1[j].astype(BF16),
                         ffn_w3[j].astype(BF16), ffn_w2[j].astype(BF16), next_g, h_dtype)
        else:
            xr, h = _moe(xr, fo2, go2, wo, ln2_g[layer], router_w[j], ew1, ew3, ew2, j, next_g,
                         h_dtype)
    return h.reshape(b, t, d)
```

```python
import functools

import jax
import jax.numpy as jnp
from jax import lax
from jax.experimental import pallas as pl
from jax.experimental.pallas import tpu as pltpu

F32 = jnp.float32
BF16 = jnp.bfloat16

FOX_HEADS = 8
FOX_HEAD_DIM = 64
FOX_W = FOX_HEADS * FOX_HEAD_DIM
GDN_HEADS = 4
GDN_DK = 128
GDN_DV = 128
GDN_W = GDN_HEADS * GDN_DK
CONV_K = 4
GDN_CHUNK = 64
N_EXPERTS = 8
MOE_BLOCK = 512
EPS = 1e-6

LANES = 128
NEG_BIG = -1e30
LOG2E = 1.4426950408889634
MIB = 1024 * 1024

REST_QKV = 3 * GDN_W
REST_GATE = REST_QKV + GDN_W
REST_W = REST_GATE + LANES
GATE_BLK = REST_GATE // LANES
G_FOX = 0
G_DEC = FOX_HEADS
G_BETA = FOX_HEADS + GDN_HEADS
G_ROWS = FOX_HEADS + 2 * GDN_HEADS


def _dot(a, b):
    return jnp.dot(a, b, preferred_element_type=F32)


def _dot_nt(a, b):
    return lax.dot_general(a, b, (((1,), (1,)), ((), ())), preferred_element_type=F32)


def _dot_tn(a, b):
    return lax.dot_general(a, b, (((0,), (0,)), ((), ())), preferred_element_type=F32)


def _params(semantics, vmem_mib):
    return pltpu.CompilerParams(dimension_semantics=semantics,
                                vmem_limit_bytes=vmem_mib * MIB)


def _rms(x, g):
    return x * lax.rsqrt(jnp.mean(x * x, axis=-1, keepdims=True) + EPS) * g


def _silu(x):
    return x * jax.nn.sigmoid(x)


def _resident(shape):
    nd = len(shape)
    return pl.BlockSpec(shape, lambda *_: (0,) * nd, pipeline_mode=pl.Buffered(1))


def _rms_body(x_ref, g_ref, o_ref):
    o_ref[...] = _rms(x_ref[...], g_ref[...]).astype(o_ref.dtype)


def _rmsnorm_rows(x, g, out_dtype, tm=1024):
    n, d = x.shape
    return pl.pallas_call(
        _rms_body,
        grid=(n // tm,),
        in_specs=[pl.BlockSpec((tm, d), lambda i: (i, 0)),
                  pl.BlockSpec((1, d), lambda i: (0, 0))],
        out_specs=pl.BlockSpec((tm, d), lambda i: (i, 0)),
        out_shape=jax.ShapeDtypeStruct((n, d), out_dtype),
        compiler_params=_params(("parallel",), 32),
        name="rmsnorm",
    )(x, g.reshape(1, d))


def _inproj_body(h_ref, w_ref, ofox_ref, orest_ref, *, col_chunk):
    h = h_ref[...]
    nf = ofox_ref.shape[1]
    nr = orest_ref.shape[1]
    for c in range(0, nf, col_chunk):
        ofox_ref[:, c:c + col_chunk] = _dot(h, w_ref[:, c:c + col_chunk]).astype(ofox_ref.dtype)
    for c in range(0, nr, col_chunk):
        e = min(c + col_chunk, nr)
        orest_ref[:, c:e] = _dot(h, w_ref[:, nf + c:nf + e])


def _inproj(h, w_all, tm=512):
    n, d = h.shape
    nf, nr = 3 * FOX_W, REST_W
    return pl.pallas_call(
        functools.partial(_inproj_body, col_chunk=512),
        grid=(n // tm,),
        in_specs=[pl.BlockSpec((tm, d), lambda i: (i, 0)),
                  _resident((d, nf + nr))],
        out_specs=[pl.BlockSpec((tm, nf), lambda i: (i, 0)),
                   pl.BlockSpec((tm, nr), lambda i: (i, 0))],
        out_shape=[jax.ShapeDtypeStruct((n, nf), BF16),
                   jax.ShapeDtypeStruct((n, nr), F32)],
        compiler_params=_params(("parallel",), 40),
        name="inproj",
    )(h, w_all)


def _gates_body(z_ref, p_ref, o_ref, ot_ref, ka_ref, carry_ref, *, tt):
    @pl.when(pl.program_id(1) == 0)
    def _():
        carry_ref[...] = jnp.zeros_like(carry_ref)

    z = z_ref[0] + p_ref[0:1, :]
    lane = lax.broadcasted_iota(jnp.int32, z.shape, 1)
    tail = jnp.log(1.0 + jnp.exp(-jnp.abs(z)))
    log_sig = jnp.minimum(z, 0.0) - tail
    softplus = jnp.maximum(z, 0.0) + tail
    decay = -jnp.exp(p_ref[1:2, :]) * softplus
    val = jnp.where(lane < G_DEC, log_sig, jnp.where(lane < G_BETA, decay, jax.nn.sigmoid(z)))

    ri = lax.broadcasted_iota(jnp.int32, (tt, tt), 0)
    ci = lax.broadcasted_iota(jnp.int32, (tt, tt), 1)
    tri = jnp.where(ci <= ri, 1.0, 0.0)
    blk = jnp.where((ci <= ri) & (ri // GDN_CHUNK == ci // GDN_CHUNK), 1.0, 0.0)
    both = jnp.concatenate([tri, blk], axis=0).astype(BF16)
    v_hi = val.astype(BF16)
    v_mid = (val - v_hi.astype(F32)).astype(BF16)
    v_lo = (val - v_hi.astype(F32) - v_mid.astype(F32)).astype(BF16)
    hi_mid = _dot(both, jnp.concatenate([v_hi, v_mid], axis=1))
    cums = hi_mid[:, :LANES] + hi_mid[:, LANES:] + _dot(both, v_lo)
    full_cum = cums[:tt] + carry_ref[...]
    chunk_cum = cums[tt:]
    out = jnp.where(lane < G_DEC, full_cum, jnp.where(lane < G_BETA, chunk_cum, val))
    carry_ref[...] = full_cum[tt - 1:tt, :]
    o_ref[0] = out
    ot_ref[0] = out.T[:G_ROWS, :]

    neg_c = jnp.where(lane < G_DEC, -LOG2E * full_cum, 0.0)
    hi = neg_c.astype(BF16).astype(F32)
    mid = (neg_c - hi).astype(BF16).astype(F32)
    lo = (neg_c - hi - mid).astype(BF16).astype(F32)
    aug = hi + pltpu.roll(mid, FOX_HEADS, axis=1) + pltpu.roll(lo, 2 * FOX_HEADS, axis=1)
    ka_ref[0] = aug.astype(BF16)


def _gates(rest3, gate_params, tt=512):
    b, t, _ = rest3.shape
    return pl.pallas_call(
        functools.partial(_gates_body, tt=tt),
        grid=(b, t // tt),
        in_specs=[pl.BlockSpec((1, tt, LANES), lambda i, j: (i, j, GATE_BLK)),
                  pl.BlockSpec((8, LANES), lambda i, j: (0, 0))],
        out_specs=[pl.BlockSpec((1, tt, LANES), lambda i, j: (i, j, 0)),
                   pl.BlockSpec((1, G_ROWS, tt), lambda i, j: (i, 0, j)),
                   pl.BlockSpec((1, tt, LANES), lambda i, j: (i, j, 0))],
        out_shape=[jax.ShapeDtypeStruct((b, t, LANES), F32),
                   jax.ShapeDtypeStruct((b, G_ROWS, t), F32),
                   jax.ShapeDtypeStruct((b, t, LANES), BF16)],
        scratch_shapes=[pltpu.VMEM((1, LANES), F32)],
        compiler_params=_params(("parallel", "arbitrary"), 32),
        name="gates",
    )(rest3, gate_params)


def _fox_body(q_ref, k_ref, ka_ref, v_ref, c_ref, g_ref, o_ref, vt_ref, acc_ref, s_ref, cm_ref, m_ref,
              qa_ref, *, tq):
    hp = pl.program_id(1)
    qi = pl.program_id(2)
    nq = pl.num_programs(2)
    half = FOX_HEAD_DIM
    t_total = k_ref.shape[1]
    lane_row = lax.broadcasted_iota(jnp.int32, (1, LANES), 1)
    sub = lax.broadcasted_iota(jnp.int32, (LANES, tq), 0)

    def prepare(q_tile):
        q = q_ref[0, pl.ds(pl.multiple_of(q_tile * tq, tq), tq), :]
        zero = jnp.zeros_like(q)
        for j in range(2):
            head = 2 * hp + j
            pick = (lane_row == head) | (lane_row == FOX_HEADS + head) | (lane_row == 2 * FOX_HEADS + head)
            ones = jnp.broadcast_to(jnp.where(pick, 1.0, 0.0).astype(BF16), (tq, LANES))
            qj = jnp.where(lane_row < half, q, zero) if j == 0 else jnp.where(lane_row < half, zero, q)
            qa_ref[j] = jnp.concatenate([qj, ones], axis=1).astype(F32).T.astype(BF16)

    qs = pl.multiple_of(qi * tq, tq)
    cq = [c_ref[0, 0, j:j + 1, pl.ds(qs, tq)] * LOG2E for j in range(2)]

    def scores(ki, j, diagonal):
        ks = pl.multiple_of(ki * tq, tq)
        k_aug = jnp.concatenate([k_ref[0, pl.ds(ks, tq), :], ka_ref[0, pl.ds(ks, tq), :]], axis=1)
        s = _dot(k_aug, qa_ref[j])
        if diagonal:
            ri = lax.broadcasted_iota(jnp.int32, s.shape, 0)
            ci = lax.broadcasted_iota(jnp.int32, s.shape, 1)
            s = jnp.where(ri <= ci, s, NEG_BIG)
        return s

    def stash(j, s):
        s_ref[j] = s
        cm_ref[j] = jnp.max(s, axis=0, keepdims=True)

    def consume(ki, j):
        ks = pl.multiple_of(ki * tq, tq)
        m_old = m_ref[j]
        m_new = jnp.maximum(m_old, cq[j] + cm_ref[j])
        p = jnp.exp2(s_ref[j] + (cq[j] - m_new)).astype(BF16)
        acc_ref[j] = jnp.exp2(m_old - m_new) * acc_ref[j] + _dot(vt_ref[j, :, pl.ds(ks, tq)], p)
        m_ref[j] = m_new

    def advance(ki_next, diagonal):
        for j in range(2):
            s_next = scores(ki_next, j, diagonal)
            consume(ki_next - 1, j)
            stash(j, s_next)

    @pl.when(qi == 0)
    def _():
        for c in range(t_total // tq):
            vt = v_ref[0, c * tq:(c + 1) * tq, :].astype(F32).T
            cols = slice(c * tq, (c + 1) * tq)
            vt_ref[0, :, cols] = jnp.where(sub < half, vt, jnp.where(sub == half, 1.0, 0.0)).astype(BF16)
            vt_ref[1, :, cols] = jnp.where(sub >= half, vt, jnp.where(sub == 0, 1.0, 0.0)).astype(BF16)
        prepare(0)
        for j in range(2):
            stash(j, scores(0, j, True))

    acc_ref[...] = jnp.zeros_like(acc_ref)
    m_ref[...] = jnp.full(m_ref.shape, NEG_BIG, F32)

    def steady(ki, carry):
        advance(ki + 1, False)
        return carry

    lax.fori_loop(0, qi - 1, steady, 0)

    @pl.when(qi > 0)
    def _():
        advance(qi, True)

    @pl.when(qi + 1 < nq)
    def _():
        prepare(qi + 1)
        first = [scores(0, j, False) for j in range(2)]
        for j in range(2):
            consume(qi, j)
        for j in range(2):
            stash(j, first[j])

    @pl.when(qi + 1 == nq)
    def _():
        for j in range(2):
            consume(qi, j)

    a0 = acc_ref[0]
    a1 = acc_ref[1]
    ot = jnp.where(sub < half, a0 / a0[half:half + 1, :], a1 / a1[0:1, :])
    o = ot.T
    lo = lane_row < half
    sq = o * o
    ms0 = jnp.sum(jnp.where(lo, sq, 0.0), axis=1, keepdims=True) / half
    ms1 = jnp.sum(jnp.where(lo, 0.0, sq), axis=1, keepdims=True) / half
    inv = lax.rsqrt(jnp.where(lo, ms0, ms1) + EPS)
    o_ref[0] = (o * inv * g_ref[...]).astype(o_ref.dtype)


def _fox(fox3, gates_t, k_aug, norm_g, tq=512):
    b, t, _ = fox3.shape
    npair = FOX_HEADS // 2
    c4 = gates_t.reshape(b, G_ROWS // 2, 2, t)
    return pl.pallas_call(
        functools.partial(_fox_body, tq=tq),
        grid=(b, npair, t // tq),
        in_specs=[pl.BlockSpec((1, t, LANES), lambda i, p, j: (i, 0, p)),
                  pl.BlockSpec((1, t, LANES), lambda i, p, j: (i, 0, npair + p)),
                  pl.BlockSpec((1, t, LANES), lambda i, p, j: (i, 0, 0)),
                  pl.BlockSpec((1, t, LANES), lambda i, p, j: (i, 0, 2 * npair + p)),
                  pl.BlockSpec((1, 1, 2, t), lambda i, p, j: (i, p, 0, 0)),
                  pl.BlockSpec((1, LANES), lambda i, p, j: (0, p))],
        out_specs=pl.BlockSpec((1, tq, LANES), lambda i, p, j: (i, j, p)),
        out_shape=jax.ShapeDtypeStruct((b, t, FOX_W), BF16),
        scratch_shapes=[pltpu.VMEM((2, LANES, t), BF16),
                        pltpu.VMEM((2, LANES, tq), F32),
                        pltpu.VMEM((2, tq, tq), F32),
                        pltpu.VMEM((2, 1, tq), F32),
                        pltpu.VMEM((2, 1, tq), F32),
                        pltpu.VMEM((2, 2 * LANES, tq), BF16)],
        compiler_params=_params(("parallel", "parallel", "arbitrary"), 48),
        name="fox_attention",
    )(fox3, fox3, k_aug, fox3, c4, norm_g.reshape(1, FOX_W))


SUPER = 256
GDN_HEADS_PER_STEP = 4
HALO = 8


def _split_bf16(x):
    hi = x.astype(BF16)
    return hi, (x - hi.astype(F32)).astype(BF16)


def _dot_split(a, b, dot=_dot):
    ah, al = a
    bh, bl = b
    return dot(ah, bh) + dot(ah, bl) + dot(al, bh)


def _gdn_body(xq_ref, xk_ref, xv_ref, z_ref, gc_ref, gr_ref, wq_ref, wk_ref, wv_ref, ng_ref,
              o_ref, s_ref, halo_ref, buf_ref, obuf_ref, *, tt):
    hps = GDN_HEADS_PER_STEP
    pair = pl.program_id(1)

    @pl.when(pl.program_id(2) == 0)
    def _():
        s_ref[...] = jnp.zeros_like(s_ref)
        halo_ref[...] = jnp.zeros_like(halo_ref)

    def conv_silu(x_ref, hh, idx, w_ref):
        lanes = slice(hh * LANES, (hh + 1) * LANES)
        slot = hh * 3 + idx
        x = x_ref[0, :, lanes]
        buf_ref[slot, 0:HALO, :] = halo_ref[slot]
        buf_ref[slot, HALO:HALO + tt, :] = x
        halo_ref[slot] = x[tt - HALO:tt, :]
        y = jnp.zeros_like(x)
        for j in range(CONV_K):
            off = HALO - (CONV_K - 1) + j
            y = y + w_ref[j:j + 1, lanes] * buf_ref[slot, off:off + tt, :]
        return _silu(y)

    gates = gc_ref[0]
    lane_t = lax.broadcasted_iota(jnp.int32, gates.shape, 1)
    ri = lax.broadcasted_iota(jnp.int32, (SUPER, SUPER), 0)
    ci = lax.broadcasted_iota(jnp.int32, (SUPER, SUPER), 1)
    same = (ri // GDN_CHUNK) == (ci // GDN_CHUNK)
    tril = same & (ci <= ri)
    strict = same & (ci < ri)
    eye = jnp.where(ri == ci, 1.0, 0.0).astype(F32)

    heads = []
    for hh in range(hps):
        head = hps * pair + hh
        q = conv_silu(xq_ref, hh, 0, wq_ref)
        k = conv_silu(xk_ref, hh, 1, wk_ref)
        v = conv_silu(xv_ref, hh, 2, wv_ref)
        q = q * lax.rsqrt(jnp.sum(q * q, axis=-1, keepdims=True) + EPS) * (GDN_DK ** -0.5)
        k = k * lax.rsqrt(jnp.sum(k * k, axis=-1, keepdims=True) + EPS)
        gcol = jnp.sum(jnp.where(lane_t == G_DEC + head, gates, 0.0), axis=1, keepdims=True)
        bcol = jnp.sum(jnp.where(lane_t == G_BETA + head, gates, 0.0), axis=1, keepdims=True)
        eg = jnp.exp(gcol)
        kb = k * bcol
        heads.append(dict(hh=hh, k=k, gcol=gcol, grow=gr_ref[0, hh], kb_split=_split_bf16(kb),
                          k_split=_split_bf16(k), k16=k.astype(BF16), q16=q.astype(BF16),
                          rhs16=jnp.concatenate([v * bcol, kb * eg], axis=1).astype(BF16),
                          qe=q * eg))

    blocks = []
    for hd in heads:
        for sc in range(tt // SUPER):
            rows = slice(sc * SUPER, (sc + 1) * SUPER)
            diff = hd["gcol"][rows] - hd["grow"][:, rows]
            decay = jnp.where(tril, jnp.exp(jnp.where(tril, diff, 0.0)), 0.0)
            gram = _dot_split(tuple(x[rows] for x in hd["kb_split"]),
                              tuple(x[rows] for x in hd["k_split"]), _dot_nt)
            a = jnp.where(strict, gram * decay, 0.0)
            blocks.append(dict(hd=hd, rows=rows, decay=decay, inv=eye - a, pw=a.astype(BF16)))

    for _ in range(5):
        for blk in blocks:
            blk["pw"] = _dot(blk["pw"], blk["pw"]).astype(BF16)
        for blk in blocks:
            blk["inv"] = blk["inv"] + _dot(blk["inv"].astype(BF16), blk["pw"])

    chunks = SUPER // GDN_CHUNK
    for blk in blocks:
        hd, rows = blk["hd"], blk["rows"]
        inv_hi, inv_lo = _split_bf16(blk["inv"])
        uw16 = (_dot(inv_hi, hd["rhs16"][rows]) + _dot(inv_lo, hd["rhs16"][rows])).astype(BF16)
        intra = jnp.where(tril, _dot_nt(hd["q16"][rows], hd["k16"][rows]) * blk["decay"], 0.0)
        iuw = _dot(intra.astype(BF16), uw16)
        blk["o_fixed"] = iuw[:, :GDN_DV]
        blk["q_eff"] = (hd["qe"][rows] - iuw[:, GDN_DV:]).astype(BF16)
        blk["s_decay"], blk["s_add"], blk["s_mix"] = [], [], []
        for c in range(chunks):
            lr = slice(c * GDN_CHUNK, (c + 1) * GDN_CHUNK)
            gr = slice(rows.start + lr.start, rows.start + lr.stop)
            g_last = hd["gcol"][gr.stop - 1:gr.stop, :]
            k_dec = (hd["k"][gr] * jnp.exp(g_last - hd["gcol"][gr])).astype(BF16)
            kuw = _dot_tn(k_dec, uw16[lr])
            blk["s_decay"].append(jnp.exp(g_last))
            blk["s_add"].append(kuw[:, :GDN_DV])
            blk["s_mix"].append(kuw[:, GDN_DV:].astype(BF16))

    for sc in range(tt // SUPER):
        for c in range(chunks):
            lr = slice(c * GDN_CHUNK, (c + 1) * GDN_CHUNK)
            gr = slice(sc * SUPER + lr.start, sc * SUPER + lr.stop)
            for blk in blocks:
                if blk["rows"].start != sc * SUPER:
                    continue
                hh = blk["hd"]["hh"]
                s = s_ref[hh]
                s16 = s.astype(BF16)
                s_ref[hh] = s * blk["s_decay"][c] + blk["s_add"][c] - _dot(blk["s_mix"][c], s16)
                obuf_ref[hh, gr, :] = blk["o_fixed"][lr] + _dot(blk["q_eff"][lr], s16)

    for hh in range(hps):
        lanes = slice(hh * LANES, (hh + 1) * LANES)
        o = _rms(obuf_ref[hh], ng_ref[...]) * _silu(z_ref[0, :, lanes])
        o_ref[0, :, lanes] = o.astype(o_ref.dtype)


def _gdn(rest3, gates3, gates_t, conv_w, norm_g, tt=512):
    b, t, _ = rest3.shape
    hps = GDN_HEADS_PER_STEP
    npair = GDN_HEADS // hps
    wide = hps * LANES
    gr4 = gates_t.reshape(b, G_ROWS, 1, t)
    x_spec = lambda off: pl.BlockSpec((1, tt, wide), lambda i, p, j: (i, j, off + p))
    w_spec = lambda off: pl.BlockSpec((CONV_K, wide), lambda i, p, j: (0, off + p))
    return pl.pallas_call(
        functools.partial(_gdn_body, tt=tt),
        grid=(b, npair, t // tt),
        in_specs=[x_spec(0), x_spec(npair), x_spec(2 * npair), x_spec(3 * npair),
                  pl.BlockSpec((1, tt, LANES), lambda i, p, j: (i, j, 0)),
                  pl.BlockSpec((1, hps, 1, tt), lambda i, p, j: (i, G_DEC // hps + p, 0, j)),
                  w_spec(0), w_spec(npair), w_spec(2 * npair),
                  pl.BlockSpec((1, LANES), lambda i, p, j: (0, 0))],
        out_specs=pl.BlockSpec((1, tt, wide), lambda i, p, j: (i, j, p)),
        out_shape=jax.ShapeDtypeStruct((b, t, GDN_W), BF16),
        scratch_shapes=[pltpu.VMEM((hps, GDN_DK, GDN_DV), F32),
                        pltpu.VMEM((hps * 3, HALO, LANES), F32),
                        pltpu.VMEM((hps * 3, tt + HALO, LANES), F32),
                        pltpu.VMEM((hps, tt, LANES), F32)],
        compiler_params=_params(("parallel", "parallel", "arbitrary"), 48),
        name="gated_delta",
    )(rest3, rest3, rest3, rest3, gates3, gr4, conv_w, conv_w, conv_w, norm_g.reshape(1, GDN_DV))


def _mixer_residual(x_ref, fo_ref, go_ref, wo_ref):
    y = _dot(fo_ref[...], wo_ref[0:FOX_W, :]) + _dot(go_ref[...], wo_ref[FOX_W:FOX_W + GDN_W, :])
    return x_ref[...] + y


def _ffn_body(x_ref, fo_ref, go_ref, wo_ref, g2_ref, w1_ref, w3_ref, w2_ref, g_ref,
              xo_ref, ho_ref, acc_ref, *, f_chunk):
    xm = _mixer_residual(x_ref, fo_ref, go_ref, wo_ref)
    h = _rms(xm, g2_ref[...]).astype(BF16)
    d_ff = w1_ref.shape[1]
    for i, c in enumerate(range(0, d_ff, f_chunk)):
        a = _dot(h, w1_ref[:, c:c + f_chunk])
        b = _dot(h, w3_ref[:, c:c + f_chunk])
        y = _dot((_silu(a) * b).astype(BF16), w2_ref[c:c + f_chunk, :])
        if i == 0:
            acc_ref[...] = y
        else:
            acc_ref[...] += y
    xn = xm + acc_ref[...]
    xo_ref[...] = xn
    ho_ref[...] = _rms(xn, g_ref[...]).astype(ho_ref.dtype)


def _ffn(x, fo, go, w_out, ln2_g, w1, w3, w2, next_g, h_dtype, tm=512, f_chunk=256):
    n, d = x.shape
    d_ff = w1.shape[1]
    row = lambda w: pl.BlockSpec((tm, w), lambda i: (i, 0))
    vec = pl.BlockSpec((1, d), lambda i: (0, 0))
    return pl.pallas_call(
        functools.partial(_ffn_body, f_chunk=f_chunk),
        grid=(n // tm,),
        in_specs=[row(d), row(FOX_W), row(GDN_W), _resident((FOX_W + GDN_W, d)), vec,
                  _resident((d, d_ff)), _resident((d, d_ff)), _resident((d_ff, d)), vec],
        out_specs=[row(d), row(d)],
        out_shape=[jax.ShapeDtypeStruct((n, d), F32), jax.ShapeDtypeStruct((n, d), h_dtype)],
        scratch_shapes=[pltpu.VMEM((tm, d), F32)],
        compiler_params=_params(("parallel",), 54),
        name="outproj_dense_swiglu",
    )(x, fo, go, w_out, ln2_g.reshape(1, d), w1, w3, w2, next_g.reshape(1, d))


R_E0, R_E1, R_W0, R_W1, R_RANK0, R_RANK1 = 0, 1, 2, 3, 4, 5


def _router_body(x_ref, fo_ref, go_ref, wo_ref, g_ref, w_ref, xo_ref, info_ref, cnt_ref, carry_ref,
                 *, tm):
    @pl.when(pl.program_id(0) == 0)
    def _():
        carry_ref[...] = jnp.zeros_like(carry_ref)

    xm = _mixer_residual(x_ref, fo_ref, go_ref, wo_ref)
    xo_ref[...] = xm
    h = _rms(xm, g_ref[...])
    h_hi, h_lo = _split_bf16(h)
    w_hi, w_lo = _split_bf16(w_ref[...])
    two = _dot(h_hi, jnp.concatenate([w_hi, w_lo], axis=1))
    logits = two[:, :LANES] + two[:, LANES:] + _dot(h_lo, w_hi)
    lane = lax.broadcasted_iota(jnp.int32, logits.shape, 1)
    valid = lane < N_EXPERTS
    lm = jnp.where(valid, logits, NEG_BIG)
    ex = jnp.exp(lm - jnp.max(lm, axis=1, keepdims=True))
    probs = jnp.where(valid, ex / jnp.sum(ex, axis=1, keepdims=True), -1.0)
    p0 = jnp.max(probs, axis=1, keepdims=True)
    e0 = jnp.min(jnp.where(probs == p0, lane, LANES), axis=1, keepdims=True)
    rest = jnp.where(lane == e0, -1.0, probs)
    p1 = jnp.max(rest, axis=1, keepdims=True)
    e1 = jnp.min(jnp.where(rest == p1, lane, LANES), axis=1, keepdims=True)
    denom = p0 + p1

    picked = jnp.where((lane == e0) | (lane == e1), 1.0, 0.0).astype(F32)
    ri = lax.broadcasted_iota(jnp.int32, (tm, tm), 0)
    ci = lax.broadcasted_iota(jnp.int32, (tm, tm), 1)
    before = jnp.where(ci < ri, 1.0, 0.0).astype(BF16)
    earlier = _dot(before, picked.astype(BF16)) + carry_ref[...]
    rank0 = jnp.sum(jnp.where(lane == e0, earlier, 0.0), axis=1, keepdims=True)
    rank1 = jnp.sum(jnp.where(lane == e1, earlier, 0.0), axis=1, keepdims=True)
    carry_ref[...] += jnp.sum(picked, axis=0, keepdims=True)

    info = jnp.zeros(logits.shape, F32)
    for col, val in ((R_E0, e0.astype(F32)), (R_E1, e1.astype(F32)), (R_W0, p0 / denom),
                     (R_W1, p1 / denom), (R_RANK0, rank0), (R_RANK1, rank1)):
        info = jnp.where(lane == col, val, info)
    info_ref[...] = info
    cnt_ref[...] = jnp.broadcast_to(carry_ref[...], cnt_ref.shape)


def _router(x, fo, go, w_out, ln_g, router_w_pad, tm=512):
    n, d = x.shape
    row = lambda w: pl.BlockSpec((tm, w), lambda i: (i, 0))
    return pl.pallas_call(
        functools.partial(_router_body, tm=tm),
        grid=(n // tm,),
        in_specs=[row(d), row(FOX_W), row(GDN_W), _resident((FOX_W + GDN_W, d)),
                  pl.BlockSpec((1, d), lambda i: (0, 0)),
                  pl.BlockSpec((d, LANES), lambda i: (0, 0))],
        out_specs=[row(d), row(LANES),
                   pl.BlockSpec((8, LANES), lambda i: (0, 0))],
        out_shape=[jax.ShapeDtypeStruct((n, d), F32),
                   jax.ShapeDtypeStruct((n, LANES), F32),
                   jax.ShapeDtypeStruct((8, LANES), F32)],
        scratch_shapes=[pltpu.VMEM((1, LANES), F32)],
        compiler_params=_params(("arbitrary",), 32),
        name="outproj_moe_router",
    )(x, fo, go, w_out, ln_g.reshape(1, d), router_w_pad)


def _row_copy(src_ref, src_row, dst_ref, dst_row, sem):
    return pltpu.make_async_copy(src_ref.at[pl.ds(src_row, 1)], dst_ref.at[pl.ds(dst_row, 1)], sem)


def _dispatch_body(pos_ref, tail_ref, x_ref, g_ref, xs_ref, buf_ref, sem_ref, zero_sem_ref, *, tm):
    i = pl.program_id(0)
    slot = i % 2

    @pl.when(i == 0)
    def _():
        buf_ref[1] = jnp.zeros(buf_ref.shape[1:], F32)
        first_spare = xs_ref.shape[0] - N_EXPERTS * MOE_BLOCK

        def fill(block_row, part):
            start = pl.multiple_of(block_row + part * tm, tm)
            return pltpu.make_async_copy(buf_ref.at[1], xs_ref.at[pl.ds(start, tm)], zero_sem_ref)

        parts = range(MOE_BLOCK // tm)
        for e in range(N_EXPERTS):
            @pl.when(tail_ref[e] >= 0)
            def _():
                for part in parts:
                    fill(tail_ref[e], part).start()
        for e in range(N_EXPERTS):
            @pl.when(tail_ref[e] >= 0)
            def _():
                for part in parts:
                    fill(tail_ref[e], part).wait()
        spare = [fill(first_spare + e * MOE_BLOCK, part) for e in range(N_EXPERTS) for part in parts]
        for cp in spare:
            cp.start()
        for cp in spare:
            cp.wait()

    buf_ref[slot] = _rms(x_ref[...], g_ref[...])

    def issue(r, carry):
        base = 2 * (i * tm + r)
        for j in range(2):
            _row_copy(buf_ref.at[slot], r, xs_ref, pos_ref[base + j], sem_ref.at[slot]).start()
        return carry

    lax.fori_loop(0, tm, issue, 0, unroll=8)

    def wait_slot(s):
        for _ in range(2):
            pltpu.make_async_copy(buf_ref.at[s], xs_ref.at[pl.ds(0, tm)], sem_ref.at[s]).wait()

    @pl.when(i > 0)
    def _():
        wait_slot(1 - slot)

    @pl.when(i == pl.num_programs(0) - 1)
    def _():
        wait_slot(slot)


def _dispatch(x, ln_g, pos_flat, last_block_row, n_slots, tm=256):
    n, d = x.shape
    grid_spec = pltpu.PrefetchScalarGridSpec(
        num_scalar_prefetch=2,
        grid=(n // tm,),
        in_specs=[pl.BlockSpec((tm, d), lambda i, pos, tail: (i, 0)),
                  pl.BlockSpec((1, d), lambda i, pos, tail: (0, 0))],
        out_specs=pl.BlockSpec(memory_space=pl.ANY),
        scratch_shapes=[pltpu.VMEM((2, tm, d), F32), pltpu.SemaphoreType.DMA((2,)),
                        pltpu.SemaphoreType.DMA],
    )
    return pl.pallas_call(
        functools.partial(_dispatch_body, tm=tm),
        grid_spec=grid_spec,
        out_shape=jax.ShapeDtypeStruct((n_slots, d), F32),
        compiler_params=_params(("arbitrary",), 32),
        name="moe_dispatch",
    )(pos_flat, last_block_row, x, ln_g.reshape(1, d))


def _experts_body(be_ref, nu_ref, xs_ref, w1_ref, w3_ref, w2_ref, o_ref, xb_ref, *, f_sub):
    del be_ref
    b = pl.program_id(0)
    f = pl.program_id(1)
    used = b < nu_ref[0]

    @pl.when(f == 0)
    def _():
        o_ref[...] = jnp.zeros_like(o_ref)

    @pl.when(used & (f == 0))
    def _():
        xb_ref[...] = xs_ref[...].astype(BF16)

    @pl.when(used)
    def _():
        xb = xb_ref[...]
        for c in range(0, w1_ref.shape[1], f_sub):
            a = _dot(xb, w1_ref[:, c:c + f_sub])
            g = _dot(xb, w3_ref[:, c:c + f_sub])
            o_ref[...] += _dot((_silu(a) * g).astype(BF16), w2_ref[c:c + f_sub, :])


def _experts(xs, block_e, n_used, w1, w3, w2, moe_layer, f_chunk=1792, f_sub=256):
    p, d = xs.shape
    d_ff = w1.shape[3]
    nb = p // MOE_BLOCK
    grid_spec = pltpu.PrefetchScalarGridSpec(
        num_scalar_prefetch=2,
        grid=(nb, d_ff // f_chunk),
        in_specs=[pl.BlockSpec((MOE_BLOCK, d), lambda b, f, be, nu: (jnp.minimum(b, nu[0] - 1), 0)),
                  pl.BlockSpec((None, None, d, f_chunk), lambda b, f, be, nu: (moe_layer, be[b], 0, f)),
                  pl.BlockSpec((None, None, d, f_chunk), lambda b, f, be, nu: (moe_layer, be[b], 0, f)),
                  pl.BlockSpec((None, None, f_chunk, d), lambda b, f, be, nu: (moe_layer, be[b], f, 0))],
        out_specs=pl.BlockSpec((MOE_BLOCK, d), lambda b, f, be, nu: (b, 0)),
        scratch_shapes=[pltpu.VMEM((MOE_BLOCK, d), BF16)],
    )
    return pl.pallas_call(
        functools.partial(_experts_body, f_sub=f_sub),
        grid_spec=grid_spec,
        out_shape=jax.ShapeDtypeStruct((p, d), F32),
        compiler_params=_params(("arbitrary", "arbitrary"), 48),
        name="moe_experts",
    )(block_e, n_used, xs, w1, w3, w2)


def _combine_body(pos_ref, x_ref, info_ref, g_ref, ys_ref, xo_ref, ho_ref, y_ref, sem_ref, *, tm):
    i = pl.program_id(0)
    slot = i % 2

    def gather(step, into):
        def issue(r, carry):
            base = 2 * (step * tm + r)
            for j in range(2):
                _row_copy(ys_ref, pos_ref[base + j], y_ref.at[into, j], r, sem_ref.at[into, j]).start()
            return carry

        lax.fori_loop(0, tm, issue, 0, unroll=8)

    @pl.when(i == 0)
    def _():
        gather(i, slot)

    @pl.when(i + 1 < pl.num_programs(0))
    def _():
        gather(i + 1, 1 - slot)

    for j in range(2):
        pltpu.make_async_copy(ys_ref.at[pl.ds(0, tm)], y_ref.at[slot, j], sem_ref.at[slot, j]).wait()

    info = info_ref[...]
    w0 = info[:, R_W0:R_W0 + 1]
    w1 = info[:, R_W1:R_W1 + 1]
    xn = x_ref[...] + (w0 * y_ref[slot, 0] + w1 * y_ref[slot, 1])
    xo_ref[...] = xn
    ho_ref[...] = _rms(xn, g_ref[...]).astype(ho_ref.dtype)


def _combine(x, info, ys, pos_flat, next_g, h_dtype, tm=256):
    n, d = x.shape
    grid_spec = pltpu.PrefetchScalarGridSpec(
        num_scalar_prefetch=1,
        grid=(n // tm,),
        in_specs=[pl.BlockSpec((tm, d), lambda i, pos: (i, 0)),
                  pl.BlockSpec((tm, LANES), lambda i, pos: (i, 0)),
                  pl.BlockSpec((1, d), lambda i, pos: (0, 0)),
                  pl.BlockSpec(memory_space=pl.ANY)],
        out_specs=[pl.BlockSpec((tm, d), lambda i, pos: (i, 0)),
                   pl.BlockSpec((tm, d), lambda i, pos: (i, 0))],
        scratch_shapes=[pltpu.VMEM((2, 2, tm, d), F32), pltpu.SemaphoreType.DMA((2, 2))],
    )
    return pl.pallas_call(
        functools.partial(_combine_body, tm=tm),
        grid_spec=grid_spec,
        out_shape=[jax.ShapeDtypeStruct((n, d), F32), jax.ShapeDtypeStruct((n, d), h_dtype)],
        compiler_params=_params(("arbitrary",), 32),
        name="moe_combine",
    )(pos_flat, x, info, next_g.reshape(1, d), ys)


def _moe(x, fo, go, w_out, ln_g, router_w, w1, w3, w2, moe_layer, next_g, h_dtype):
    n, d = x.shape
    router_w_pad = jnp.pad(router_w, ((0, 0), (0, LANES - N_EXPERTS)))
    x, info, cnt = _router(x, fo, go, w_out, ln_g, router_w_pad)

    counts = cnt[0, :N_EXPERTS].astype(jnp.int32)
    padded = ((counts + MOE_BLOCK - 1) // MOE_BLOCK) * MOE_BLOCK
    pend = jnp.cumsum(padded)
    pstart = pend - padded
    experts = info[:, R_E0:R_E1 + 1].astype(jnp.int32)
    ranks = info[:, R_RANK0:R_RANK1 + 1].astype(jnp.int32)
    pos_flat = (pstart[experts] + ranks).reshape(-1)
    n_blocks = -(-(2 * n) // MOE_BLOCK) + N_EXPERTS
    block_row = jnp.arange(n_blocks, dtype=jnp.int32) * MOE_BLOCK
    block_e = jnp.minimum(jnp.sum(pend[None, :] <= block_row[:, None], axis=1),
                          N_EXPERTS - 1).astype(jnp.int32)
    n_used = (pend[-1:] // MOE_BLOCK).astype(jnp.int32)
    last_block_row = jnp.where(padded > 0, pend - MOE_BLOCK, -1).astype(jnp.int32)

    xs = _dispatch(x, ln_g, pos_flat, last_block_row, n_blocks * MOE_BLOCK)
    ys = _experts(xs, block_e, n_used, w1, w3, w2, moe_layer)
    return _combine(x, info, ys, pos_flat, next_g, h_dtype)


def _in_weights(w_in):
    o = 3 * FOX_W
    ff = w_in[:, o:o + FOX_HEADS]
    o += FOX_HEADS
    gqkv = w_in[:, o:o + REST_QKV]
    o += REST_QKV
    gab = w_in[:, o:o + 2 * GDN_HEADS]
    o += 2 * GDN_HEADS
    gz = w_in[:, o:o + GDN_W]
    fox = w_in[:, :3 * FOX_W]
    fox = jnp.concatenate([fox[:, :FOX_W] * (LOG2E * FOX_HEAD_DIM ** -0.5), fox[:, FOX_W:]], axis=1)
    pad = jnp.zeros((w_in.shape[0], LANES - G_ROWS), w_in.dtype)
    return jnp.concatenate([fox, gqkv, gz, ff, gab, pad], axis=1).astype(BF16)


def _gate_params(f_bias, dt_bias, a_log):
    row0 = jnp.zeros((LANES,), F32).at[G_FOX:G_FOX + FOX_HEADS].set(f_bias)
    row0 = row0.at[G_DEC:G_DEC + GDN_HEADS].set(dt_bias)
    row1 = jnp.zeros((LANES,), F32).at[G_DEC:G_DEC + GDN_HEADS].set(a_log)
    return jnp.zeros((8, LANES), F32).at[0].set(row0).at[1].set(row1)


def kernel(x, ln1_g, w_in, fox_f_bias, fox_norm_g, gdn_conv_w, gdn_a_log, gdn_dt_bias, gdn_norm_g,
           w_out, ln2_g, ffn_w1, ffn_w3, ffn_w2, router_w, exp_w1, exp_w3, exp_w2, final_g):
    b, t, d = x.shape
    n = b * t
    depth = w_in.shape[0]
    xr = x.reshape(n, d)
    ew1, ew3, ew2 = exp_w1.astype(BF16), exp_w3.astype(BF16), exp_w2.astype(BF16)
    h = _rmsnorm_rows(xr, ln1_g[0], BF16)
    for layer in range(depth):
        fox2, rest2 = _inproj(h, _in_weights(w_in[layer]))
        fox3 = fox2.reshape(b, t, 3 * FOX_W)
        rest3 = rest2.reshape(b, t, REST_W)
        gates3, gates_t, k_aug = _gates(rest3, _gate_params(fox_f_bias[layer], gdn_dt_bias[layer],
                                                            gdn_a_log[layer]))
        fo = _fox(fox3, gates_t, k_aug, fox_norm_g[layer])
        go = _gdn(rest3, gates3, gates_t, gdn_conv_w[layer], gdn_norm_g[layer])
        fo2, go2 = fo.reshape(n, FOX_W), go.reshape(n, GDN_W)
        wo = w_out[layer].astype(BF16)
        last = layer == depth - 1
        next_g = final_g if last else ln1_g[layer + 1]
        h_dtype = F32 if last else BF16
        j = layer // 2
        if layer % 2 == 0:
            xr, h = _ffn(xr, fo2, go2, wo, ln2_g[layer], ffn_w1[j].astype(BF16),
                         ffn_w3[j].astype(BF16), ffn_w2[j].astype(BF16), next_g, h_dtype)
        else:
            xr, h = _moe(xr, fo2, go2, wo, ln2_g[layer], router_w[j], ew1, ew3, ew2, j, next_g,
                         h_dtype)
    return h.reshape(b, t, d)
```

```python
import functools

import jax
import jax.numpy as jnp
from jax import lax
from jax.experimental import pallas as pl
from jax.experimental.pallas import tpu as pltpu

F32 = jnp.float32
BF16 = jnp.bfloat16

FOX_HEADS = 8
FOX_HEAD_DIM = 64
FOX_W = FOX_HEADS * FOX_HEAD_DIM
GDN_HEADS = 4
GDN_DK = 128
GDN_DV = 128
GDN_W = GDN_HEADS * GDN_DK
CONV_K = 4
GDN_CHUNK = 64
N_EXPERTS = 8
MOE_BLOCK = 512
EPS = 1e-6

LANES = 128
NEG_BIG = -1e30
LOG2E = 1.4426950408889634
MIB = 1024 * 1024

REST_QKV = 3 * GDN_W
REST_GATE = REST_QKV + GDN_W
REST_W = REST_GATE + LANES
GATE_BLK = REST_GATE // LANES
G_FOX = 0
G_DEC = FOX_HEADS
G_BETA = FOX_HEADS + GDN_HEADS
G_ROWS = FOX_HEADS + 2 * GDN_HEADS


def _dot(a, b):
    return jnp.dot(a, b, preferred_element_type=F32)


def _dot_nt(a, b):
    return lax.dot_general(a, b, (((1,), (1,)), ((), ())), preferred_element_type=F32)


def _dot_tn(a, b):
    return lax.dot_general(a, b, (((0,), (0,)), ((), ())), preferred_element_type=F32)


def _params(semantics, vmem_mib):
    return pltpu.CompilerParams(dimension_semantics=semantics,
                                vmem_limit_bytes=vmem_mib * MIB)


def _rms(x, g):
    return x * lax.rsqrt(jnp.mean(x * x, axis=-1, keepdims=True) + EPS) * g


def _silu(x):
    return x * jax.nn.sigmoid(x)


def _resident(shape):
    nd = len(shape)
    return pl.BlockSpec(shape, lambda *_: (0,) * nd, pipeline_mode=pl.Buffered(1))


def _rms_body(x_ref, g_ref, o_ref):
    o_ref[...] = _rms(x_ref[...], g_ref[...]).astype(o_ref.dtype)


def _rmsnorm_rows(x, g, out_dtype, tm=1024):
    n, d = x.shape
    return pl.pallas_call(
        _rms_body,
        grid=(n // tm,),
        in_specs=[pl.BlockSpec((tm, d), lambda i: (i, 0)),
                  pl.BlockSpec((1, d), lambda i: (0, 0))],
        out_specs=pl.BlockSpec((tm, d), lambda i: (i, 0)),
        out_shape=jax.ShapeDtypeStruct((n, d), out_dtype),
        compiler_params=_params(("parallel",), 32),
        name="rmsnorm",
    )(x, g.reshape(1, d))


def _inproj_body(h_ref, w_ref, ofox_ref, orest_ref, *, col_chunk):
    h = h_ref[...]
    nf = ofox_ref.shape[1]
    nr = orest_ref.shape[1]
    for c in range(0, nf, col_chunk):
        ofox_ref[:, c:c + col_chunk] = _dot(h, w_ref[:, c:c + col_chunk]).astype(ofox_ref.dtype)
    for c in range(0, nr, col_chunk):
        e = min(c + col_chunk, nr)
        orest_ref[:, c:e] = _dot(h, w_ref[:, nf + c:nf + e])


def _inproj(h, w_all, tm=512):
    n, d = h.shape
    nf, nr = 3 * FOX_W, REST_W
    return pl.pallas_call(
        functools.partial(_inproj_body, col_chunk=512),
        grid=(n // tm,),
        in_specs=[pl.BlockSpec((tm, d), lambda i: (i, 0)),
                  _resident((d, nf + nr))],
        out_specs=[pl.BlockSpec((tm, nf), lambda i: (i, 0)),
                   pl.BlockSpec((tm, nr), lambda i: (i, 0))],
        out_shape=[jax.ShapeDtypeStruct((n, nf), BF16),
                   jax.ShapeDtypeStruct((n, nr), F32)],
        compiler_params=_params(("parallel",), 40),
        name="inproj",
    )(h, w_all)


def _gates_body(z_ref, p_ref, o_ref, ot_ref, ka_ref, carry_ref, *, tt):
    @pl.when(pl.program_id(1) == 0)
    def _():
        carry_ref[...] = jnp.zeros_like(carry_ref)

    z = z_ref[0] + p_ref[0:1, :]
    lane = lax.broadcasted_iota(jnp.int32, z.shape, 1)
    tail = jnp.log(1.0 + jnp.exp(-jnp.abs(z)))
    log_sig = jnp.minimum(z, 0.0) - tail
    softplus = jnp.maximum(z, 0.0) + tail
    decay = -jnp.exp(p_ref[1:2, :]) * softplus
    val = jnp.where(lane < G_DEC, log_sig, jnp.where(lane < G_BETA, decay, jax.nn.sigmoid(z)))

    ri = lax.broadcasted_iota(jnp.int32, (tt, tt), 0)
    ci = lax.broadcasted_iota(jnp.int32, (tt, tt), 1)
    tri = jnp.where(ci <= ri, 1.0, 0.0)
    blk = jnp.where((ci <= ri) & (ri // GDN_CHUNK == ci // GDN_CHUNK), 1.0, 0.0)
    both = jnp.concatenate([tri, blk], axis=0).astype(BF16)
    v_hi = val.astype(BF16)
    v_mid = (val - v_hi.astype(F32)).astype(BF16)
    v_lo = (val - v_hi.astype(F32) - v_mid.astype(F32)).astype(BF16)
    hi_mid = _dot(both, jnp.concatenate([v_hi, v_mid], axis=1))
    cums = hi_mid[:, :LANES] + hi_mid[:, LANES:] + _dot(both, v_lo)
    full_cum = cums[:tt] + carry_ref[...]
    chunk_cum = cums[tt:]
    out = jnp.where(lane < G_DEC, full_cum, jnp.where(lane < G_BETA, chunk_cum, val))
    carry_ref[...] = full_cum[tt - 1:tt, :]
    o_ref[0] = out
    ot_ref[0] = out.T[:G_ROWS, :]

    neg_c = jnp.where(lane < G_DEC, -LOG2E * full_cum, 0.0)
    hi = neg_c.astype(BF16).astype(F32)
    mid = (neg_c - hi).astype(BF16).astype(F32)
    lo = (neg_c - hi - mid).astype(BF16).astype(F32)
    aug = hi + pltpu.roll(mid, FOX_HEADS, axis=1) + pltpu.roll(lo, 2 * FOX_HEADS, axis=1)
    ka_ref[0] = aug.astype(BF16)


def _gates(rest3, gate_params, tt=512):
    b, t, _ = rest3.shape
    return pl.pallas_call(
        functools.partial(_gates_body, tt=tt),
        grid=(b, t // tt),
        in_specs=[pl.BlockSpec((1, tt, LANES), lambda i, j: (i, j, GATE_BLK)),
                  pl.BlockSpec((8, LANES), lambda i, j: (0, 0))],
        out_specs=[pl.BlockSpec((1, tt, LANES), lambda i, j: (i, j, 0)),
                   pl.BlockSpec((1, G_ROWS, tt), lambda i, j: (i, 0, j)),
                   pl.BlockSpec((1, tt, LANES), lambda i, j: (i, j, 0))],
        out_shape=[jax.ShapeDtypeStruct((b, t, LANES), F32),
                   jax.ShapeDtypeStruct((b, G_ROWS, t), F32),
                   jax.ShapeDtypeStruct((b, t, LANES), BF16)],
        scratch_shapes=[pltpu.VMEM((1, LANES), F32)],
        compiler_params=_params(("parallel", "arbitrary"), 32),
        name="gates",
    )(rest3, gate_params)


def _fox_body(q_ref, k_ref, ka_ref, v_ref, c_ref, g_ref, o_ref, vt_ref, acc_ref, s_ref, cm_ref, m_ref,
              qa_ref, *, tq):
    hp = pl.program_id(1)
    qi = pl.program_id(2)
    nq = pl.num_programs(2)
    half = FOX_HEAD_DIM
    t_total = k_ref.shape[1]
    lane_row = lax.broadcasted_iota(jnp.int32, (1, LANES), 1)
    sub = lax.broadcasted_iota(jnp.int32, (LANES, tq), 0)

    def prepare(q_tile):
        q = q_ref[0, pl.ds(pl.multiple_of(q_tile * tq, tq), tq), :]
        zero = jnp.zeros_like(q)
        for j in range(2):
            head = 2 * hp + j
            pick = (lane_row == head) | (lane_row == FOX_HEADS + head) | (lane_row == 2 * FOX_HEADS + head)
            ones = jnp.broadcast_to(jnp.where(pick, 1.0, 0.0).astype(BF16), (tq, LANES))
            qj = jnp.where(lane_row < half, q, zero) if j == 0 else jnp.where(lane_row < half, zero, q)
            qa_ref[j] = jnp.concatenate([qj, ones], axis=1).astype(F32).T.astype(BF16)

    qs = pl.multiple_of(qi * tq, tq)
    cq = [c_ref[0, 0, j:j + 1, pl.ds(qs, tq)] * LOG2E for j in range(2)]

    def scores(ki, j, diagonal):
        ks = pl.multiple_of(ki * tq, tq)
        k_aug = jnp.concatenate([k_ref[0, pl.ds(ks, tq), :], ka_ref[0, pl.ds(ks, tq), :]], axis=1)
        s = _dot(k_aug, qa_ref[j])
        if diagonal:
            ri = lax.broadcasted_iota(jnp.int32, s.shape, 0)
            ci = lax.broadcasted_iota(jnp.int32, s.shape, 1)
            s = jnp.where(ri <= ci, s, NEG_BIG)
        return s

    def stash(j, s):
        s_ref[j] = s
        cm_ref[j] = jnp.max(s, axis=0, keepdims=True)

    def consume(ki, j):
        ks = pl.multiple_of(ki * tq, tq)
        m_old = m_ref[j]
        m_new = jnp.maximum(m_old, cq[j] + cm_ref[j])
        p = jnp.exp2(s_ref[j] + (cq[j] - m_new)).astype(BF16)
        acc_ref[j] = jnp.exp2(m_old - m_new) * acc_ref[j] + _dot(vt_ref[j, :, pl.ds(ks, tq)], p)
        m_ref[j] = m_new

    def advance(ki_next, diagonal):
        for j in range(2):
            s_next = scores(ki_next, j, diagonal)
            consume(ki_next - 1, j)
            stash(j, s_next)

    @pl.when(qi == 0)
    def _():
        for c in range(t_total // tq):
            vt = v_ref[0, c * tq:(c + 1) * tq, :].astype(F32).T
            cols = slice(c * tq, (c + 1) * tq)
            vt_ref[0, :, cols] = jnp.where(sub < half, vt, jnp.where(sub == half, 1.0, 0.0)).astype(BF16)
            vt_ref[1, :, cols] = jnp.where(sub >= half, vt, jnp.where(sub == 0, 1.0, 0.0)).astype(BF16)
        prepare(0)
        for j in range(2):
            stash(j, scores(0, j, True))

    acc_ref[...] = jnp.zeros_like(acc_ref)
    m_ref[...] = jnp.full(m_ref.shape, NEG_BIG, F32)

    def steady(ki, carry):
        advance(ki + 1, False)
        return carry

    lax.fori_loop(0, qi - 1, steady, 0)

    @pl.when(qi > 0)
    def _():
        advance(qi, True)

    @pl.when(qi + 1 < nq)
    def _():
        prepare(qi + 1)
        first = [scores(0, j, False) for j in range(2)]
        for j in range(2):
            consume(qi, j)
        for j in range(2):
            stash(j, first[j])

    @pl.when(qi + 1 == nq)
    def _():
        for j in range(2):
            consume(qi, j)

    a0 = acc_ref[0]
    a1 = acc_ref[1]
    ot = jnp.where(sub < half, a0 / a0[half:half + 1, :], a1 / a1[0:1, :])
    o = ot.T
    lo = lane_row < half
    sq = o * o
    ms0 = jnp.sum(jnp.where(lo, sq, 0.0), axis=1, keepdims=True) / half
    ms1 = jnp.sum(jnp.where(lo, 0.0, sq), axis=1, keepdims=True) / half
    inv = lax.rsqrt(jnp.where(lo, ms0, ms1) + EPS)
    o_ref[0] = (o * inv * g_ref[...]).astype(o_ref.dtype)


def _fox(fox3, gates_t, k_aug, norm_g, tq=512):
    b, t, _ = fox3.shape
    npair = FOX_HEADS // 2
    c4 = gates_t.reshape(b, G_ROWS // 2, 2, t)
    return pl.pallas_call(
        functools.partial(_fox_body, tq=tq),
        grid=(b, npair, t // tq),
        in_specs=[pl.BlockSpec((1, t, LANES), lambda i, p, j: (i, 0, p)),
                  pl.BlockSpec((1, t, LANES), lambda i, p, j: (i, 0, npair + p)),
                  pl.BlockSpec((1, t, LANES), lambda i, p, j: (i, 0, 0)),
                  pl.BlockSpec((1, t, LANES), lambda i, p, j: (i, 0, 2 * npair + p)),
                  pl.BlockSpec((1, 1, 2, t), lambda i, p, j: (i, p, 0, 0)),
                  pl.BlockSpec((1, LANES), lambda i, p, j: (0, p))],
        out_specs=pl.BlockSpec((1, tq, LANES), lambda i, p, j: (i, j, p)),
        out_shape=jax.ShapeDtypeStruct((b, t, FOX_W), BF16),
        scratch_shapes=[pltpu.VMEM((2, LANES, t), BF16),
                        pltpu.VMEM((2, LANES, tq), F32),
                        pltpu.VMEM((2, tq, tq), F32),
                        pltpu.VMEM((2, 1, tq), F32),
                        pltpu.VMEM((2, 1, tq), F32),
                        pltpu.VMEM((2, 2 * LANES, tq), BF16)],
        compiler_params=_params(("parallel", "parallel", "arbitrary"), 48),
        name="fox_attention",
    )(fox3, fox3, k_aug, fox3, c4, norm_g.reshape(1, FOX_W))


SUPER = 256
GDN_HEADS_PER_STEP = 4
HALO = 8


def _split_bf16(x):
    hi = x.astype(BF16)
    return hi, (x - hi.astype(F32)).astype(BF16)


def _dot_split(a, b, dot=_dot):
    ah, al = a
    bh, bl = b
    return dot(ah, bh) + dot(ah, bl) + dot(al, bh)


def _gdn_body(xq_ref, xk_ref, xv_ref, z_ref, gc_ref, gr_ref, wq_ref, wk_ref, wv_ref, ng_ref,
              o_ref, s_ref, halo_ref, buf_ref, obuf_ref, *, tt):
    hps = GDN_HEADS_PER_STEP
    pair = pl.program_id(1)

    @pl.when(pl.program_id(2) == 0)
    def _():
        s_ref[...] = jnp.zeros_like(s_ref)
        halo_ref[...] = jnp.zeros_like(halo_ref)

    def conv_silu(x_ref, hh, idx, w_ref):
        lanes = slice(hh * LANES, (hh + 1) * LANES)
        slot = hh * 3 + idx
        x = x_ref[0, :, lanes]
        buf_ref[slot, 0:HALO, :] = halo_ref[slot]
        buf_ref[slot, HALO:HALO + tt, :] = x
        halo_ref[slot] = x[tt - HALO:tt, :]
        y = jnp.zeros_like(x)
        for j in range(CONV_K):
            off = HALO - (CONV_K - 1) + j
            y = y + w_ref[j:j + 1, lanes] * buf_ref[slot, off:off + tt, :]
        return _silu(y)

    gates = gc_ref[0]
    lane_t = lax.broadcasted_iota(jnp.int32, gates.shape, 1)
    ri = lax.broadcasted_iota(jnp.int32, (SUPER, SUPER), 0)
    ci = lax.broadcasted_iota(jnp.int32, (SUPER, SUPER), 1)
    same = (ri // GDN_CHUNK) == (ci // GDN_CHUNK)
    tril = same & (ci <= ri)
    strict = same & (ci < ri)
    eye = jnp.where(ri == ci, 1.0, 0.0).astype(F32)

    heads = []
    for hh in range(hps):
        head = hps * pair + hh
        q = conv_silu(xq_ref, hh, 0, wq_ref)
        k = conv_silu(xk_ref, hh, 1, wk_ref)
        v = conv_silu(xv_ref, hh, 2, wv_ref)
        q = q * lax.rsqrt(jnp.sum(q * q, axis=-1, keepdims=True) + EPS) * (GDN_DK ** -0.5)
        k = k * lax.rsqrt(jnp.sum(k * k, axis=-1, keepdims=True) + EPS)
        gcol = jnp.sum(jnp.where(lane_t == G_DEC + head, gates, 0.0), axis=1, keepdims=True)
        bcol = jnp.sum(jnp.where(lane_t == G_BETA + head, gates, 0.0), axis=1, keepdims=True)
        eg = jnp.exp(gcol)
        kb = k * bcol
        heads.append(dict(hh=hh, k=k, gcol=gcol, grow=gr_ref[0, hh], kb_split=_split_bf16(kb),
                          k_split=_split_bf16(k), k16=k.astype(BF16), q16=q.astype(BF16),
                          rhs16=jnp.concatenate([v * bcol, kb * eg], axis=1).astype(BF16),
                          qe=q * eg))

    blocks = []
    for hd in heads:
        for sc in range(tt // SUPER):
            rows = slice(sc * SUPER, (sc + 1) * SUPER)
            diff = hd["gcol"][rows] - hd["grow"][:, rows]
            decay = jnp.where(tril, jnp.exp(jnp.where(tril, diff, 0.0)), 0.0)
            gram = _dot_split(tuple(x[rows] for x in hd["kb_split"]),
                              tuple(x[rows] for x in hd["k_split"]), _dot_nt)
            a = jnp.where(strict, gram * decay, 0.0)
            blocks.append(dict(hd=hd, rows=rows, decay=decay, inv=eye - a, pw=a.astype(BF16)))

    for _ in range(5):
        for blk in blocks:
            blk["pw"] = _dot(blk["pw"], blk["pw"]).astype(BF16)
        for blk in blocks:
            blk["inv"] = blk["inv"] + _dot(blk["inv"].astype(BF16), blk["pw"])

    chunks = SUPER // GDN_CHUNK
    for blk in blocks:
        hd, rows = blk["hd"], blk["rows"]
        inv_hi, inv_lo = _split_bf16(blk["inv"])
        uw16 = (_dot(inv_hi, hd["rhs16"][rows]) + _dot(inv_lo, hd["rhs16"][rows])).astype(BF16)
        intra = jnp.where(tril, _dot_nt(hd["q16"][rows], hd["k16"][rows]) * blk["decay"], 0.0)
        iuw = _dot(intra.astype(BF16), uw16)
        blk["o_fixed"] = iuw[:, :GDN_DV]
        blk["q_eff"] = (hd["qe"][rows] - iuw[:, GDN_DV:]).astype(BF16)
        blk["s_decay"], blk["s_add"], blk["s_mix"] = [], [], []
        for c in range(chunks):
            lr = slice(c * GDN_CHUNK, (c + 1) * GDN_CHUNK)
            gr = slice(rows.start + lr.start, rows.start + lr.stop)
            g_last = hd["gcol"][gr.stop - 1:gr.stop, :]
            k_dec = (hd["k"][gr] * jnp.exp(g_last - hd["gcol"][gr])).astype(BF16)
            kuw = _dot_tn(k_dec, uw16[lr])
            blk["s_decay"].append(jnp.exp(g_last))
            blk["s_add"].append(kuw[:, :GDN_DV])
            blk["s_mix"].append(kuw[:, GDN_DV:].astype(BF16))

    for sc in range(tt // SUPER):
        for c in range(chunks):
            lr = slice(c * GDN_CHUNK, (c + 1) * GDN_CHUNK)
            gr = slice(sc * SUPER + lr.start, sc * SUPER + lr.stop)
            for blk in blocks:
                if blk["rows"].start != sc * SUPER:
                    continue
                hh = blk["hd"]["hh"]
                s = s_ref[hh]
                s16 = s.astype(BF16)
                s_ref[hh] = s * blk["s_decay"][c] + blk["s_add"][c] - _dot(blk["s_mix"][c], s16)
                obuf_ref[hh, gr, :] = blk["o_fixed"][lr] + _dot(blk["q_eff"][lr], s16)

    for hh in range(hps):
        lanes = slice(hh * LANES, (hh + 1) * LANES)
        o = _rms(obuf_ref[hh], ng_ref[...]) * _silu(z_ref[0, :, lanes])
        o_ref[0, :, lanes] = o.astype(o_ref.dtype)


def _gdn(rest3, gates3, gates_t, conv_w, norm_g, tt=512):
    b, t, _ = rest3.shape
    hps = GDN_HEADS_PER_STEP
    npair = GDN_HEADS // hps
    wide = hps * LANES
    gr4 = gates_t.reshape(b, G_ROWS, 1, t)
    x_spec = lambda off: pl.BlockSpec((1, tt, wide), lambda i, p, j: (i, j, off + p))
    w_spec = lambda off: pl.BlockSpec((CONV_K, wide), lambda i, p, j: (0, off + p))
    return pl.pallas_call(
        functools.partial(_gdn_body, tt=tt),
        grid=(b, npair, t // tt),
        in_specs=[x_spec(0), x_spec(npair), x_spec(2 * npair), x_spec(3 * npair),
                  pl.BlockSpec((1, tt, LANES), lambda i, p, j: (i, j, 0)),
                  pl.BlockSpec((1, hps, 1, tt), lambda i, p, j: (i, G_DEC // hps + p, 0, j)),
                  w_spec(0), w_spec(npair), w_spec(2 * npair),
                  pl.BlockSpec((1, LANES), lambda i, p, j: (0, 0))],
        out_specs=pl.BlockSpec((1, tt, wide), lambda i, p, j: (i, j, p)),
        out_shape=jax.ShapeDtypeStruct((b, t, GDN_W), BF16),
        scratch_shapes=[pltpu.VMEM((hps, GDN_DK, GDN_DV), F32),
                        pltpu.VMEM((hps * 3, HALO, LANES), F32),
                        pltpu.VMEM((hps * 3, tt + HALO, LANES), F32),
                        pltpu.VMEM((hps, tt, LANES), F32)],
        compiler_params=_params(("parallel", "parallel", "arbitrary"), 48),
        name="gated_delta",
    )(rest3, rest3, rest3, rest3, gates3, gr4, conv_w, conv_w, conv_w, norm_g.reshape(1, GDN_DV))


def _mixer_residual(x_ref, fo_ref, go_ref, wo_ref):
    y = _dot(fo_ref[...], wo_ref[0:FOX_W, :]) + _dot(go_ref[...], wo_ref[FOX_W:FOX_W + GDN_W, :])
    return x_ref[...] + y


def _ffn_body(x_ref, fo_ref, go_ref, wo_ref, g2_ref, w1_ref, w3_ref, w2_ref, g_ref,
              xo_ref, ho_ref, acc_ref, *, f_chunk):
    xm = _mixer_residual(x_ref, fo_ref, go_ref, wo_ref)
    h = _rms(xm, g2_ref[...]).astype(BF16)
    d_ff = w1_ref.shape[1]
    for i, c in enumerate(range(0, d_ff, f_chunk)):
        a = _dot(h, w1_ref[:, c:c + f_chunk])
        b = _dot(h, w3_ref[:, c:c + f_chunk])
        y = _dot((_silu(a) * b).astype(BF16), w2_ref[c:c + f_chunk, :])
        if i == 0:
            acc_ref[...] = y
        else:
            acc_ref[...] += y
    xn = xm + acc_ref[...]
    xo_ref[...] = xn
    ho_ref[...] = _rms(xn, g_ref[...]).astype(ho_ref.dtype)


def _ffn(x, fo, go, w_out, ln2_g, w1, w3, w2, next_g, h_dtype, tm=512, f_chunk=256):
    n, d = x.shape
    d_ff = w1.shape[1]
    row = lambda w: pl.BlockSpec((tm, w), lambda i: (i, 0))
    vec = pl.BlockSpec((1, d), lambda i: (0, 0))
    return pl.pallas_call(
        functools.partial(_ffn_body, f_chunk=f_chunk),
        grid=(n // tm,),
        in_specs=[row(d), row(FOX_W), row(GDN_W), _resident((FOX_W + GDN_W, d)), vec,
                  _resident((d, d_ff)), _resident((d, d_ff)), _resident((d_ff, d)), vec],
        out_specs=[row(d), row(d)],
        out_shape=[jax.ShapeDtypeStruct((n, d), F32), jax.ShapeDtypeStruct((n, d), h_dtype)],
        scratch_shapes=[pltpu.VMEM((tm, d), F32)],
        compiler_params=_params(("parallel",), 54),
        name="outproj_dense_swiglu",
    )(x, fo, go, w_out, ln2_g.reshape(1, d), w1, w3, w2, next_g.reshape(1, d))


R_E0, R_E1, R_W0, R_W1, R_RANK0, R_RANK1 = 0, 1, 2, 3, 4, 5


def _router_body(x_ref, fo_ref, go_ref, wo_ref, g_ref, w_ref, xo_ref, info_ref, cnt_ref, carry_ref,
                 *, tm):
    @pl.when(pl.program_id(0) == 0)
    def _():
        carry_ref[...] = jnp.zeros_like(carry_ref)

    xm = _mixer_residual(x_ref, fo_ref, go_ref, wo_ref)
    xo_ref[...] = xm
    h = _rms(xm, g_ref[...])
    h_hi, h_lo = _split_bf16(h)
    w_hi, w_lo = _split_bf16(w_ref[...])
    two = _dot(h_hi, jnp.concatenate([w_hi, w_lo], axis=1))
    logits = two[:, :LANES] + two[:, LANES:] + _dot(h_lo, w_hi)
    lane = lax.broadcasted_iota(jnp.int32, logits.shape, 1)
    valid = lane < N_EXPERTS
    lm = jnp.where(valid, logits, NEG_BIG)
    ex = jnp.exp(lm - jnp.max(lm, axis=1, keepdims=True))
    probs = jnp.where(valid, ex / jnp.sum(ex, axis=1, keepdims=True), -1.0)
    p0 = jnp.max(probs, axis=1, keepdims=True)
    e0 = jnp.min(jnp.where(probs == p0, lane, LANES), axis=1, keepdims=True)
    rest = jnp.where(lane == e0, -1.0, probs)
    p1 = jnp.max(rest, axis=1, keepdims=True)
    e1 = jnp.min(jnp.where(rest == p1, lane, LANES), axis=1, keepdims=True)
    denom = p0 + p1

    picked = jnp.where((lane == e0) | (lane == e1), 1.0, 0.0).astype(F32)
    ri = lax.broadcasted_iota(jnp.int32, (tm, tm), 0)
    ci = lax.broadcasted_iota(jnp.int32, (tm, tm), 1)
    before = jnp.where(ci < ri, 1.0, 0.0).astype(BF16)
    earlier = _dot(before, picked.astype(BF16)) + carry_ref[...]
    rank0 = jnp.sum(jnp.where(lane == e0, earlier, 0.0), axis=1, keepdims=True)
    rank1 = jnp.sum(jnp.where(lane == e1, earlier, 0.0), axis=1, keepdims=True)
    carry_ref[...] += jnp.sum(picked, axis=0, keepdims=True)

    info = jnp.zeros(logits.shape, F32)
    for col, val in ((R_E0, e0.astype(F32)), (R_E1, e1.astype(F32)), (R_W0, p0 / denom),
                     (R_W1, p1 / denom), (R_RANK0, rank0), (R_RANK1, rank1)):
        info = jnp.where(lane == col, val, info)
    info_ref[...] = info
    cnt_ref[...] = jnp.broadcast_to(carry_ref[...], cnt_ref.shape)


def _router(x, fo, go, w_out, ln_g, router_w_pad, tm=512):
    n, d = x.shape
    row = lambda w: pl.BlockSpec((tm, w), lambda i: (i, 0))
    return pl.pallas_call(
        functools.partial(_router_body, tm=tm),
        grid=(n // tm,),
        in_specs=[row(d), row(FOX_W), row(GDN_W), _resident((FOX_W + GDN_W, d)),
                  pl.BlockSpec((1, d), lambda i: (0, 0)),
                  pl.BlockSpec((d, LANES), lambda i: (0, 0))],
        out_specs=[row(d), row(LANES),
                   pl.BlockSpec((8, LANES), lambda i: (0, 0))],
        out_shape=[jax.ShapeDtypeStruct((n, d), F32),
                   jax.ShapeDtypeStruct((n, LANES), F32),
                   jax.ShapeDtypeStruct((8, LANES), F32)],
        scratch_shapes=[pltpu.VMEM((1, LANES), F32)],
        compiler_params=_params(("arbitrary",), 32),
        name="outproj_moe_router",
    )(x, fo, go, w_out, ln_g.reshape(1, d), router_w_pad)


def _row_copy(src_ref, src_row, dst_ref, dst_row, sem):
    return pltpu.make_async_copy(src_ref.at[pl.ds(src_row, 1)], dst_ref.at[pl.ds(dst_row, 1)], sem)


def _dispatch_body(pos_ref, tail_ref, x_ref, g_ref, xs_ref, buf_ref, sem_ref, zero_sem_ref, *, tm):
    i = pl.program_id(0)
    slot = i % 2

    @pl.when(i == 0)
    def _():
        buf_ref[1] = jnp.zeros(buf_ref.shape[1:], F32)
        first_spare = xs_ref.shape[0] - N_EXPERTS * MOE_BLOCK

        def fill(block_row, part):
            start = pl.multiple_of(block_row + part * tm, tm)
            return pltpu.make_async_copy(buf_ref.at[1], xs_ref.at[pl.ds(start, tm)], zero_sem_ref)

        parts = range(MOE_BLOCK // tm)
        for e in range(N_EXPERTS):
            @pl.when(tail_ref[e] >= 0)
            def _():
                for part in parts:
                    fill(tail_ref[e], part).start()
        for e in range(N_EXPERTS):
            @pl.when(tail_ref[e] >= 0)
            def _():
                for part in parts:
                    fill(tail_ref[e], part).wait()
        spare = [fill(first_spare + e * MOE_BLOCK, part) for e in range(N_EXPERTS) for part in parts]
        for cp in spare:
            cp.start()
        for cp in spare:
            cp.wait()

    buf_ref[slot] = _rms(x_ref[...], g_ref[...])

    def issue(r, carry):
        base = 2 * (i * tm + r)
        for j in range(2):
            _row_copy(buf_ref.at[slot], r, xs_ref, pos_ref[base + j], sem_ref.at[slot]).start(priority=j)
        return carry

    lax.fori_loop(0, tm, issue, 0, unroll=8)

    def wait_slot(s):
        for _ in range(2):
            pltpu.make_async_copy(buf_ref.at[s], xs_ref.at[pl.ds(0, tm)], sem_ref.at[s]).wait()

    @pl.when(i > 0)
    def _():
        wait_slot(1 - slot)

    @pl.when(i == pl.num_programs(0) - 1)
    def _():
        wait_slot(slot)


def _dispatch(x, ln_g, pos_flat, last_block_row, n_slots, tm=256):
    n, d = x.shape
    grid_spec = pltpu.PrefetchScalarGridSpec(
        num_scalar_prefetch=2,
        grid=(n // tm,),
        in_specs=[pl.BlockSpec((tm, d), lambda i, pos, tail: (i, 0)),
                  pl.BlockSpec((1, d), lambda i, pos, tail: (0, 0))],
        out_specs=pl.BlockSpec(memory_space=pl.ANY),
        scratch_shapes=[pltpu.VMEM((2, tm, d), F32), pltpu.SemaphoreType.DMA((2,)),
                        pltpu.SemaphoreType.DMA],
    )
    return pl.pallas_call(
        functools.partial(_dispatch_body, tm=tm),
        grid_spec=grid_spec,
        out_shape=jax.ShapeDtypeStruct((n_slots, d), F32),
        compiler_params=_params(("arbitrary",), 32),
        name="moe_dispatch",
    )(pos_flat, last_block_row, x, ln_g.reshape(1, d))


def _experts_body(be_ref, nu_ref, xs_ref, w1_ref, w3_ref, w2_ref, o_ref, xb_ref, *, f_sub):
    del be_ref
    b = pl.program_id(0)
    f = pl.program_id(1)
    used = b < nu_ref[0]

    @pl.when(f == 0)
    def _():
        o_ref[...] = jnp.zeros_like(o_ref)

    @pl.when(used & (f == 0))
    def _():
        xb_ref[...] = xs_ref[...].astype(BF16)

    @pl.when(used)
    def _():
        xb = xb_ref[...]
        for c in range(0, w1_ref.shape[1], f_sub):
            a = _dot(xb, w1_ref[:, c:c + f_sub])
            g = _dot(xb, w3_ref[:, c:c + f_sub])
            o_ref[...] += _dot((_silu(a) * g).astype(BF16), w2_ref[c:c + f_sub, :])


def _experts(xs, block_e, n_used, w1, w3, w2, moe_layer, f_chunk=1792, f_sub=256):
    p, d = xs.shape
    d_ff = w1.shape[3]
    nb = p // MOE_BLOCK
    grid_spec = pltpu.PrefetchScalarGridSpec(
        num_scalar_prefetch=2,
        grid=(nb, d_ff // f_chunk),
        in_specs=[pl.BlockSpec((MOE_BLOCK, d), lambda b, f, be, nu: (jnp.minimum(b, nu[0] - 1), 0)),
                  pl.BlockSpec((None, None, d, f_chunk), lambda b, f, be, nu: (moe_layer, be[b], 0, f)),
                  pl.BlockSpec((None, None, d, f_chunk), lambda b, f, be, nu: (moe_layer, be[b], 0, f)),
                  pl.BlockSpec((None, None, f_chunk, d), lambda b, f, be, nu: (moe_layer, be[b], f, 0))],
        out_specs=pl.BlockSpec((MOE_BLOCK, d), lambda b, f, be, nu: (b, 0)),
        scratch_shapes=[pltpu.VMEM((MOE_BLOCK, d), BF16)],
    )
    return pl.pallas_call(
        functools.partial(_experts_body, f_sub=f_sub),
        grid_spec=grid_spec,
        out_shape=jax.ShapeDtypeStruct((p, d), F32),
        compiler_params=_params(("arbitrary", "arbitrary"), 48),
        name="moe_experts",
    )(block_e, n_used, xs, w1, w3, w2)


def _combine_body(pos_ref, x_ref, info_ref, g_ref, ys_ref, xo_ref, ho_ref, y_ref, sem_ref, *, tm):
    i = pl.program_id(0)
    slot = i % 2

    def gather(step, into):
        def issue(r, carry):
            base = 2 * (step * tm + r)
            for j in range(2):
                _row_copy(ys_ref, pos_ref[base + j], y_ref.at[into, j], r,
                          sem_ref.at[into, j]).start(priority=j)
            return carry

        lax.fori_loop(0, tm, issue, 0, unroll=8)

    @pl.when(i == 0)
    def _():
        gather(i, slot)

    @pl.when(i + 1 < pl.num_programs(0))
    def _():
        gather(i + 1, 1 - slot)

    for j in range(2):
        pltpu.make_async_copy(ys_ref.at[pl.ds(0, tm)], y_ref.at[slot, j], sem_ref.at[slot, j]).wait()

    info = info_ref[...]
    w0 = info[:, R_W0:R_W0 + 1]
    w1 = info[:, R_W1:R_W1 + 1]
    xn = x_ref[...] + (w0 * y_ref[slot, 0] + w1 * y_ref[slot, 1])
    xo_ref[...] = xn
    ho_ref[...] = _rms(xn, g_ref[...]).astype(ho_ref.dtype)


def _combine(x, info, ys, pos_flat, next_g, h_dtype, tm=256):
    n, d = x.shape
    grid_spec = pltpu.PrefetchScalarGridSpec(
        num_scalar_prefetch=1,
        grid=(n // tm,),
        in_specs=[pl.BlockSpec((tm, d), lambda i, pos: (i, 0)),
                  pl.BlockSpec((tm, LANES), lambda i, pos: (i, 0)),
                  pl.BlockSpec((1, d), lambda i, pos: (0, 0)),
                  pl.BlockSpec(memory_space=pl.ANY)],
        out_specs=[pl.BlockSpec((tm, d), lambda i, pos: (i, 0)),
                   pl.BlockSpec((tm, d), lambda i, pos: (i, 0))],
        scratch_shapes=[pltpu.VMEM((2, 2, tm, d), F32), pltpu.SemaphoreType.DMA((2, 2))],
    )
    return pl.pallas_call(
        functools.partial(_combine_body, tm=tm),
        grid_spec=grid_spec,
        out_shape=[jax.ShapeDtypeStruct((n, d), F32), jax.ShapeDtypeStruct((n, d), h_dtype)],
        compiler_params=_params(("arbitrary",), 32),
        name="moe_combine",
    )(pos_flat, x, info, next_g.reshape(1, d), ys)


def _moe(x, fo, go, w_out, ln_g, router_w, w1, w3, w2, moe_layer, next_g, h_dtype):
    n, d = x.shape
    router_w_pad = jnp.pad(router_w, ((0, 0), (0, LANES - N_EXPERTS)))
    x, info, cnt = _router(x, fo, go, w_out, ln_g, router_w_pad)

    counts = cnt[0, :N_EXPERTS].astype(jnp.int32)
    padded = ((counts + MOE_BLOCK - 1) // MOE_BLOCK) * MOE_BLOCK
    pend = jnp.cumsum(padded)
    pstart = pend - padded
    experts = info[:, R_E0:R_E1 + 1].astype(jnp.int32)
    ranks = info[:, R_RANK0:R_RANK1 + 1].astype(jnp.int32)
    pos_flat = (pstart[experts] + ranks).reshape(-1)
    n_blocks = -(-(2 * n) // MOE_BLOCK) + N_EXPERTS
    block_row = jnp.arange(n_blocks, dtype=jnp.int32) * MOE_BLOCK
    block_e = jnp.minimum(jnp.sum(pend[None, :] <= block_row[:, None], axis=1),
                          N_EXPERTS - 1).astype(jnp.int32)
    n_used = (pend[-1:] // MOE_BLOCK).astype(jnp.int32)
    last_block_row = jnp.where(padded > 0, pend - MOE_BLOCK, -1).astype(jnp.int32)

    xs = _dispatch(x, ln_g, pos_flat, last_block_row, n_blocks * MOE_BLOCK)
    ys = _experts(xs, block_e, n_used, w1, w3, w2, moe_layer)
    return _combine(x, info, ys, pos_flat, next_g, h_dtype)


def _in_weights(w_in):
    o = 3 * FOX_W
    ff = w_in[:, o:o + FOX_HEADS]
    o += FOX_HEADS
    gqkv = w_in[:, o:o + REST_QKV]
    o += REST_QKV
    gab = w_in[:, o:o + 2 * GDN_HEADS]
    o += 2 * GDN_HEADS
    gz = w_in[:, o:o + GDN_W]
    fox = w_in[:, :3 * FOX_W]
    fox = jnp.concatenate([fox[:, :FOX_W] * (LOG2E * FOX_HEAD_DIM ** -0.5), fox[:, FOX_W:]], axis=1)
    pad = jnp.zeros((w_in.shape[0], LANES - G_ROWS), w_in.dtype)
    return jnp.concatenate([fox, gqkv, gz, ff, gab, pad], axis=1).astype(BF16)


def _gate_params(f_bias, dt_bias, a_log):
    row0 = jnp.zeros((LANES,), F32).at[G_FOX:G_FOX + FOX_HEADS].set(f_bias)
    row0 = row0.at[G_DEC:G_DEC + GDN_HEADS].set(dt_bias)
    row1 = jnp.zeros((LANES,), F32).at[G_DEC:G_DEC + GDN_HEADS].set(a_log)
    return jnp.zeros((8, LANES), F32).at[0].set(row0).at[1].set(row1)


def kernel(x, ln1_g, w_in, fox_f_bias, fox_norm_g, gdn_conv_w, gdn_a_log, gdn_dt_bias, gdn_norm_g,
           w_out, ln2_g, ffn_w1, ffn_w3, ffn_w2, router_w, exp_w1, exp_w3, exp_w2, final_g):
    b, t, d = x.shape
    n = b * t
    depth = w_in.shape[0]
    xr = x.reshape(n, d)
    ew1, ew3, ew2 = exp_w1.astype(BF16), exp_w3.astype(BF16), exp_w2.astype(BF16)
    h = _rmsnorm_rows(xr, ln1_g[0], BF16)
    for layer in range(depth):
        fox2, rest2 = _inproj(h, _in_weights(w_in[layer]))
        fox3 = fox2.reshape(b, t, 3 * FOX_W)
        rest3 = rest2.reshape(b, t, REST_W)
        gates3, gates_t, k_aug = _gates(rest3, _gate_params(fox_f_bias[layer], gdn_dt_bias[layer],
                                                            gdn_a_log[layer]))
        fo = _fox(fox3, gates_t, k_aug, fox_norm_g[layer])
        go = _gdn(rest3, gates3, gates_t, gdn_conv_w[layer], gdn_norm_g[layer])
        fo2, go2 = fo.reshape(n, FOX_W), go.reshape(n, GDN_W)
        wo = w_out[layer].astype(BF16)
        last = layer == depth - 1
        next_g = final_g if last else ln1_g[layer + 1]
        h_dtype = F32 if last else BF16
        j = layer // 2
        if layer % 2 == 0:
            xr, h = _ffn(xr, fo2, go2, wo, ln2_g[layer], ffn_w1[j].astype(BF16),
                         ffn_w3[j].astype(BF16), ffn_w2[j].astype(BF16), next_g, h_dtype)
        else:
            xr, h = _moe(xr, fo2, go2, wo, ln2_g[layer], router_w[j], ew1, ew3, ew2, j, next_g,
                         h_dtype)
    return h.reshape(b, t, d)
```
